```python
import jax, jax.numpy as jnp
from jax import lax
import numpy as np

D_MODEL = 1024
BATCH = 8
SEQ = 2048
DEPTH = 1
DEC_BATCH = 16
DEC_SEQ = 16
PAST_LEN = 1024

CHUNK = 64
ATT_HEADS = 16
ATT_HEAD_DIM = 64
ATT_KV_HEADS = 4
ATT_WIDTH = ATT_HEADS * ATT_HEAD_DIM
KV_WIDTH = ATT_KV_HEADS * ATT_HEAD_DIM
IDX_HEADS = 4
IDX_DIM = 64
TOPK_MAX = 256
Q_BLOCK = 128
ROPE_THETA = 500000.0
ROPE_FRACTION = 4
SSM_D_INNER = 2 * D_MODEL
SSM_HEAD_DIM = 64
SSM_HEADS = SSM_D_INNER // SSM_HEAD_DIM
SSM_GROUPS = 4
SSM_D_STATE = 128
CONV_WIDTH = 4
CONV_DIM = SSM_D_INNER + 2 * SSM_GROUPS * SSM_D_STATE
N_BRANCHES = 2
IN_SPLITS = (ATT_WIDTH, KV_WIDTH, KV_WIDTH, IDX_HEADS * IDX_DIM, IDX_DIM, IDX_HEADS,
             SSM_D_INNER, CONV_DIM, SSM_HEADS, N_BRANCHES * D_MODEL)
IN_DIM = sum(IN_SPLITS)
PEER_HEADS = 8
PEER_NKEYS = 128
PEER_EXPERTS = PEER_NKEYS * PEER_NKEYS
PEER_QDIM = 256
PEER_TOPK = 16
PEER_BLOCK = 256
DN_ALPHA = (2.0 * DEPTH) ** 0.25
DN_BETA = (8.0 * DEPTH) ** -0.25
LN_EPS = 1e-5
RMS_EPS = 1e-5
NEG_INF = -1e30

kernel_name = "chunk_causal_dsa_ssd_peer_deepnorm_step"


def layer_norm(x, g, b):
    xf = x.astype(jnp.float32)
    mu = jnp.mean(xf, axis=-1, keepdims=True)
    var = jnp.mean(jnp.square(xf - mu), axis=-1, keepdims=True)
    return ((xf - mu) * lax.rsqrt(var + LN_EPS) * g + b).astype(x.dtype)


def partial_rope(x, pos):
    d = x.shape[-1]
    rd = d // ROPE_FRACTION
    half = rd // 2
    inv = ROPE_THETA ** (-(jnp.arange(half, dtype=jnp.float32) * 2.0) / rd)
    ang = pos.astype(jnp.float32)[:, None] * inv[None, :]
    cos = jnp.cos(ang)[:, None, :]
    sin = jnp.sin(ang)[:, None, :]
    xf = x.astype(jnp.float32)
    x1, x2, xp = xf[..., :half], xf[..., half:rd], xf[..., rd:]
    return jnp.concatenate([x1 * cos - x2 * sin, x2 * cos + x1 * sin, xp], axis=-1).astype(x.dtype)


def dsa_attend(q, qi, wi, q_chunk, k, v, ki, k_chunk, k_sel):
    b, nq = q.shape[0], q.shape[1]
    idx_logits = jnp.einsum('bqhd,bsd->bqhs', qi.astype(jnp.float32), ki.astype(jnp.float32))
    score = jnp.einsum('bqhs,bqh->bqs', jax.nn.relu(idx_logits), wi.astype(jnp.float32))
    admissible = k_chunk[None, :] <= q_chunk[:, None]
    score = jnp.where(admissible[None], score, NEG_INF)
    _, sel = lax.top_k(score, k_sel)
    take = jax.vmap(lambda rows, ids: rows[ids])
    kg = take(k, sel)
    vg = take(v, sel)
    valid = k_chunk[sel] <= q_chunk[None, :, None]
    qg = q.reshape(b, nq, ATT_KV_HEADS, ATT_HEADS // ATT_KV_HEADS, ATT_HEAD_DIM)
    logits = jnp.einsum('bqgrd,bqkgd->bqgrk', qg.astype(jnp.float32), kg.astype(jnp.float32)) * (ATT_HEAD_DIM ** -0.5)
    logits = jnp.where(valid[:, :, None, None, :], logits, NEG_INF)
    probs = jax.nn.softmax(logits, axis=-1)
    out = jnp.einsum('bqgrk,bqkgd->bqgrd', probs, vg.astype(jnp.float32))
    return out.reshape(b, nq, ATT_WIDTH).astype(q.dtype)


def causal_conv(xbc, prev, w, bias):
    L = xbc.shape[1]
    xpad = jnp.concatenate([prev.astype(xbc.dtype), xbc], axis=1)
    out = sum(xpad[:, j:j + L] * w[j] for j in range(CONV_WIDTH)) + bias
    return jax.nn.silu(out), xpad[:, xpad.shape[1] - (CONV_WIDTH - 1):]


def ssd(x, dt, A, Bm, Cm, init_state, chunk):
    b, L, h, p = x.shape
    g, n = Bm.shape[2], Bm.shape[3]
    r = h // g
    nc = L // chunk
    a = (dt * A).reshape(b, nc, chunk, g, r).transpose(0, 3, 4, 1, 2)
    xdt = (x * dt[..., None]).reshape(b, nc, chunk, g, r, p)
    Bc = Bm.reshape(b, nc, chunk, g, n)
    Cc = Cm.reshape(b, nc, chunk, g, n)
    a_cum = jnp.cumsum(a, axis=-1)
    seg = a_cum[..., :, None] - a_cum[..., None, :]
    tri = jnp.tril(jnp.ones((chunk, chunk), dtype=bool))
    decay_in = jnp.exp(jnp.where(tri, seg, -jnp.inf))
    cb = jnp.einsum('bclgn,bcsgn->bgcls', Cc, Bc)
    y_diag = jnp.einsum('bgcls,bgrcls,bcsgrp->bclgrp', cb, decay_in, xdt)
    decay_to_end = jnp.exp(a_cum[..., -1:] - a_cum)
    chunk_states = jnp.einsum('bcsgn,bgrcs,bcsgrp->cbgrpn', Bc, decay_to_end, xdt)
    chunk_decay = jnp.exp(a_cum[..., -1]).transpose(3, 0, 1, 2)

    def step(s, inp):
        dec, st = inp
        return s * dec[..., None, None] + st, s

    final, prev = lax.scan(step, init_state.reshape(b, g, r, p, n), (chunk_decay, chunk_states))
    y_off = jnp.einsum('bclgn,cbgrpn,bgrcl->bclgrp', Cc, prev, jnp.exp(a_cum))
    return (y_diag + y_off).reshape(b, L, h, p), final.reshape(b, h, p, n)


def ssm_branch(xbc_raw, z, dt_raw, conv_prev, ssm_prev, conv_w, conv_b, dt_bias, a_log, d_skip, norm_w):
    b, L, _ = xbc_raw.shape
    xbc, conv_new = causal_conv(xbc_raw, conv_prev, conv_w, conv_b)
    gn = SSM_GROUPS * SSM_D_STATE
    xs = xbc[..., :SSM_D_INNER].reshape(b, L, SSM_HEADS, SSM_HEAD_DIM).astype(jnp.float32)
    Bm = xbc[..., SSM_D_INNER:SSM_D_INNER + gn].reshape(b, L, SSM_GROUPS, SSM_D_STATE).astype(jnp.float32)
    Cm = xbc[..., SSM_D_INNER + gn:].reshape(b, L, SSM_GROUPS, SSM_D_STATE).astype(jnp.float32)
    dt = jax.nn.softplus(dt_raw.astype(jnp.float32) + dt_bias.astype(jnp.float32))
    A = -jnp.exp(a_log.astype(jnp.float32))
    chunk = CHUNK if L % CHUNK == 0 else L
    y, ssm_new = ssd(xs, dt, A, Bm, Cm, ssm_prev.astype(jnp.float32), chunk)
    y = y + d_skip.astype(jnp.float32)[:, None] * xs
    y = y.reshape(b, L, SSM_D_INNER) * jax.nn.silu(z.astype(jnp.float32))
    yg = y.reshape(b, L, SSM_GROUPS, SSM_D_INNER // SSM_GROUPS)
    yg = yg * lax.rsqrt(jnp.mean(jnp.square(yg), axis=-1, keepdims=True) + RMS_EPS)
    y = yg.reshape(b, L, SSM_D_INNER) * norm_w
    return y.astype(xbc_raw.dtype), conv_new, ssm_new


def token_mixer(x, pos, past, w_in, b_gate, conv_w, conv_b, dt_bias, a_log, d_skip, ssm_norm_w,
                w_attn_br, w_ssm_br, w_out):
    b, L, _ = x.shape
    proj = jnp.einsum('bld,de->ble', x, w_in)
    offs = np.cumsum(IN_SPLITS)[:-1].tolist()
    q, k, v, qi, ki, wi, z, xbc_raw, dt_raw, gate_pre = jnp.split(proj, offs, axis=-1)
    q = partial_rope(q.reshape(b, L, ATT_HEADS, ATT_HEAD_DIM), pos)
    k = partial_rope(k.reshape(b, L, ATT_KV_HEADS, ATT_HEAD_DIM), pos)
    v = v.reshape(b, L, ATT_KV_HEADS, ATT_HEAD_DIM)
    qi = partial_rope(qi.reshape(b, L, IDX_HEADS, IDX_DIM), pos)
    ki = partial_rope(ki[:, :, None, :], pos)[:, :, 0]
    if past is None:
        k_chunk = pos // CHUNK
        k_sel = min(TOPK_MAX, L // 4)
        qb = Q_BLOCK if L % Q_BLOCK == 0 else L
        nb = L // qb

        def blocks(a):
            return a.reshape((b, nb, qb) + a.shape[2:]).swapaxes(0, 1)

        attn = lax.map(lambda blk: dsa_attend(blk[0], blk[1], blk[2], blk[3] // CHUNK, k, v, ki, k_chunk, k_sel),
                       (blocks(q), blocks(qi), blocks(wi), pos.reshape(nb, qb)))
        attn = attn.swapaxes(0, 1).reshape(b, L, ATT_WIDTH)
        conv_prev = jnp.zeros((b, CONV_WIDTH - 1, CONV_DIM), x.dtype)
        ssm_prev = jnp.zeros((b, SSM_HEADS, SSM_HEAD_DIM, SSM_D_STATE), jnp.float32)
    else:
        ck, cv, cki, conv_prev, ssm_prev = past
        k_all = jnp.concatenate([ck.astype(k.dtype), k], axis=1)
        v_all = jnp.concatenate([cv.astype(v.dtype), v], axis=1)
        ki_all = jnp.concatenate([cki.astype(ki.dtype), ki], axis=1)
        n_keys = k_all.shape[1]
        k_chunk = jnp.arange(n_keys) // CHUNK
        attn = dsa_attend(q, qi, wi, pos // CHUNK, k_all, v_all, ki_all, k_chunk, min(TOPK_MAX, n_keys // 4))
    y_ssm, conv_new, ssm_new = ssm_branch(xbc_raw, z, dt_raw, conv_prev, ssm_prev, conv_w, conv_b,
                                          dt_bias, a_log, d_skip, ssm_norm_w)
    ya = jnp.einsum('ble,ed->bld', attn, w_attn_br)
    ym = jnp.einsum('ble,ed->bld', y_ssm, w_ssm_br)
    gates = jax.nn.sigmoid((gate_pre + b_gate).astype(jnp.float32)).astype(x.dtype)
    gates = gates.reshape(b, L, N_BRANCHES, D_MODEL)
    mixed = gates[:, :, 0] * ya + gates[:, :, 1] * ym
    out = jnp.einsum('bld,de->ble', mixed, w_out)
    return out, (k, v, ki, conv_new, ssm_new)


def peer(x, wq, keys1, keys2, u, v):
    b, L, d = x.shape
    t = b * L
    nb = -(-t // PEER_BLOCK)
    xf = jnp.pad(x.reshape(t, d), ((0, nb * PEER_BLOCK - t), (0, 0))).reshape(nb, PEER_BLOCK, d)
    half = PEER_QDIM // 2

    def block(xb):
        q = (xb @ wq).reshape(PEER_BLOCK, PEER_HEADS, PEER_QDIM).astype(jnp.float32)
        s1 = jnp.einsum('thd,nd->thn', q[..., :half], keys1.astype(jnp.float32))
        s2 = jnp.einsum('thd,nd->thn', q[..., half:], keys2.astype(jnp.float32))
        v1, i1 = lax.top_k(s1, PEER_TOPK)
        v2, i2 = lax.top_k(s2, PEER_TOPK)
        cand = (v1[..., :, None] + v2[..., None, :]).reshape(PEER_BLOCK, PEER_HEADS, PEER_TOPK * PEER_TOPK)
        cidx = (i1[..., :, None] * PEER_NKEYS + i2[..., None, :]).reshape(PEER_BLOCK, PEER_HEADS, PEER_TOPK * PEER_TOPK)
        top_s, top_i = lax.top_k(cand, PEER_TOPK)
        expert = jnp.take_along_axis(cidx, top_i, axis=-1)
        gate = jax.nn.softmax(top_s, axis=-1)
        ug = u[expert]
        vg = v[expert]
        act = jax.nn.gelu(jnp.einsum('td,thkd->thk', xb, ug).astype(jnp.float32), approximate=False)
        return jnp.einsum('thk,thkd->td', (gate * act).astype(xb.dtype), vg)

    out = lax.map(block, xf).reshape(nb * PEER_BLOCK, d)[:t]
    return out.reshape(b, L, d)


def trunk_layer(x, pos, past, lw):
    (w_in, b_gate, conv_w, conv_b, dt_bias, a_log, d_skip, ssm_norm_w, w_attn_br, w_ssm_br, w_out,
     ln1_g, ln1_b, peer_wq, peer_keys1, peer_keys2, peer_u, peer_v, ln2_g, ln2_b) = lw
    mix, state = token_mixer(x, pos, past, w_in, b_gate, conv_w, conv_b, dt_bias, a_log, d_skip,
                             ssm_norm_w, w_attn_br, w_ssm_br, w_out)
    h = layer_norm(DN_ALPHA * x + mix, ln1_g, ln1_b)
    h = layer_norm(DN_ALPHA * h + peer(h, peer_wq, peer_keys1, peer_keys2, peer_u, peer_v), ln2_g, ln2_b)
    return h, state


def setup_inputs(seed: int = 0) -> dict:
    key = jax.random.key(seed)
    ks = jax.random.split(key, 32)
    f32 = jnp.float32

    def nrm(k, shape, scale):
        return jax.random.normal(k, shape, f32) * scale

    dt0 = jnp.exp(jax.random.uniform(ks[10], (DEPTH, SSM_HEADS), f32, float(np.log(1e-3)), float(np.log(1e-1))))
    return {
        "x_prompt": nrm(ks[0], (BATCH, SEQ, D_MODEL), 1.0),
        "x_sample": nrm(ks[1], (DEC_BATCH, DEC_SEQ, D_MODEL), 1.0),
        "cache_k": nrm(ks[2], (DEPTH, DEC_BATCH, PAST_LEN, ATT_KV_HEADS, ATT_HEAD_DIM), 1.0),
        "cache_v": nrm(ks[3], (DEPTH, DEC_BATCH, PAST_LEN, ATT_KV_HEADS, ATT_HEAD_DIM), 1.0),
        "cache_kidx": nrm(ks[4], (DEPTH, DEC_BATCH, PAST_LEN, IDX_DIM), 1.0),
        "state_conv": nrm(ks[5], (DEPTH, DEC_BATCH, CONV_WIDTH - 1, CONV_DIM), 1.0),
        "state_ssm": nrm(ks[6], (DEPTH, DEC_BATCH, SSM_HEADS, SSM_HEAD_DIM, SSM_D_STATE), 0.5),
        "w_in": nrm(ks[7], (DEPTH, D_MODEL, IN_DIM), D_MODEL ** -0.5),
        "b_gate": nrm(ks[8], (DEPTH, N_BRANCHES * D_MODEL), 0.02),
        "conv_w": nrm(ks[9], (DEPTH, CONV_WIDTH, CONV_DIM), CONV_WIDTH ** -0.5),
        "conv_b": nrm(ks[11], (DEPTH, CONV_DIM), 0.02),
        "dt_bias": dt0 + jnp.log(-jnp.expm1(-dt0)),
        "a_log": jnp.log(jax.random.uniform(ks[12], (DEPTH, SSM_HEADS), f32, 1.0, 16.0)),
        "d_skip": 1.0 + nrm(ks[13], (DEPTH, SSM_HEADS), 0.02),
        "ssm_norm_w": 1.0 + nrm(ks[14], (DEPTH, SSM_D_INNER), 0.02),
        "w_attn_br": nrm(ks[15], (DEPTH, ATT_WIDTH, D_MODEL), ATT_WIDTH ** -0.5),
        "w_ssm_br": nrm(ks[16], (DEPTH, SSM_D_INNER, D_MODEL), SSM_D_INNER ** -0.5),
        "w_out": nrm(ks[17], (DEPTH, D_MODEL, D_MODEL), DN_BETA * D_MODEL ** -0.5),
        "ln1_g": 1.0 + nrm(ks[18], (DEPTH, D_MODEL), 0.02),
        "ln1_b": nrm(ks[19], (DEPTH, D_MODEL), 0.02),
        "peer_wq": nrm(ks[20], (DEPTH, D_MODEL, PEER_HEADS * PEER_QDIM), D_MODEL ** -0.5),
        "peer_keys1": nrm(ks[21], (DEPTH, PEER_NKEYS, PEER_QDIM // 2), (PEER_QDIM // 2) ** -0.5),
        "peer_keys2": nrm(ks[22], (DEPTH, PEER_NKEYS, PEER_QDIM // 2), (PEER_QDIM // 2) ** -0.5),
        "peer_u": nrm(ks[23], (DEPTH, PEER_EXPERTS, D_MODEL), D_MODEL ** -0.5),
        "peer_v": nrm(ks[24], (DEPTH, PEER_EXPERTS, D_MODEL), DN_BETA),
        "ln2_g": 1.0 + nrm(ks[25], (DEPTH, D_MODEL), 0.02),
        "ln2_b": nrm(ks[26], (DEPTH, D_MODEL), 0.02),
    }


def reference(x_prompt, x_sample, cache_k, cache_v, cache_kidx, state_conv, state_ssm,
              w_in, b_gate, conv_w, conv_b, dt_bias, a_log, d_skip, ssm_norm_w,
              w_attn_br, w_ssm_br, w_out, ln1_g, ln1_b,
              peer_wq, peer_keys1, peer_keys2, peer_u, peer_v, ln2_g, ln2_b):
    pos_p = jnp.arange(x_prompt.shape[1])
    pos_s = cache_k.shape[2] + jnp.arange(x_sample.shape[1])
    hp, hs = x_prompt, x_sample
    new_p, new_s = [], []
    for l in range(DEPTH):
        lw = (w_in[l], b_gate[l], conv_w[l], conv_b[l], dt_bias[l], a_log[l], d_skip[l], ssm_norm_w[l],
              w_attn_br[l], w_ssm_br[l], w_out[l], ln1_g[l], ln1_b[l],
              peer_wq[l], peer_keys1[l], peer_keys2[l], peer_u[l], peer_v[l], ln2_g[l], ln2_b[l])
        hp, sp = trunk_layer(hp, pos_p, None, lw)
        hs, ss = trunk_layer(hs, pos_s, (cache_k[l], cache_v[l], cache_kidx[l], state_conv[l], state_ssm[l]), lw)
        new_p.append(sp)
        new_s.append(ss)

    def stack(lst, i):
        return jnp.stack([e[i] for e in lst], axis=0)

    p_k, p_v, p_kidx, p_conv, p_ssm = stack(new_p, 0), stack(new_p, 1), stack(new_p, 2), stack(new_p, 3), stack(new_p, 4)
    s_k, s_v, s_kidx, s_conv, s_ssm = stack(new_s, 0), stack(new_s, 1), stack(new_s, 2), stack(new_s, 3), stack(new_s, 4)
    return (hp, hs, p_k, p_v, p_kidx, p_conv, p_ssm, s_k, s_v, s_kidx, s_conv, s_ssm)
```

```python
import functools

import jax
import jax.numpy as jnp
import numpy as np
from jax import lax
from jax.experimental import pallas as pl
from jax.experimental.pallas import tpu as pltpu

F32 = jnp.float32
BF16 = jnp.bfloat16

LANES = 128
SUBLANES = 8
VMEM_LIMIT_BYTES = 48 * 1024 * 1024

CHUNK = 64
ATT_HEADS = 16
ATT_HEAD_DIM = 64
ATT_KV_HEADS = 4
ATT_GROUP = ATT_HEADS // ATT_KV_HEADS
IDX_HEADS = 4
IDX_DIM = 64
TOPK_MAX = 256
ROPE_THETA = 500000.0
ROPE_FRACTION = 4
SSM_HEAD_DIM = 64
SSM_GROUPS = 4
SSM_D_STATE = 128
CONV_WIDTH = 4
N_BRANCHES = 2
PEER_HEADS = 8
PEER_NKEYS = 128
PEER_QDIM = 256
PEER_TOPK = 16
LN_EPS = 1e-5
RMS_EPS = 1e-5
NEG_INF = -1e30
SSD_CHUNK = 128
INT32_MIN = -(2 ** 31)


def _cparams(*sem):
    return pltpu.CompilerParams(dimension_semantics=sem, vmem_limit_bytes=VMEM_LIMIT_BYTES)


def _dot(a, b):
    return jnp.dot(a, b, preferred_element_type=F32)


def _dot_nt(a, b):
    return lax.dot_general(a, b, (((1,), (1,)), ((), ())), preferred_element_type=F32)


def _rope_tables(pos):
    rd = ATT_HEAD_DIM // ROPE_FRACTION
    half = rd // 2
    inv = ROPE_THETA ** (-(jnp.arange(half, dtype=F32) * 2.0) / rd)
    ang = pos.astype(F32)[:, None] * inv[None, :]
    cos, sin = jnp.cos(ang), jnp.sin(ang)
    n = pos.shape[0]
    ones = jnp.ones((n, ATT_HEAD_DIM - rd), F32)
    zeros = jnp.zeros((n, ATT_HEAD_DIM - rd), F32)
    zh = jnp.zeros((n, half), F32)
    cos_h = jnp.concatenate([cos, cos, ones], axis=1)
    sa_h = jnp.concatenate([-sin, zh, zeros], axis=1)
    sb_h = jnp.concatenate([zh, sin, zeros], axis=1)
    rep = LANES // ATT_HEAD_DIM
    return jnp.tile(cos_h, (1, rep)), jnp.tile(sa_h, (1, rep)), jnp.tile(sb_h, (1, rep))


def _proj_a_kernel(x_ref, w_ref, cos_ref, sa_ref, sb_ref,
                   q_ref, k_ref, v_ref, qi_ref, ki_ref, wi_ref, dt_ref):
    x = x_ref[...].astype(BF16)
    cos, sa, sb = cos_ref[...], sa_ref[...], sb_ref[...]
    half = ATT_HEAD_DIM // ROPE_FRACTION // 2

    def rope(t):
        up = pltpu.roll(t, LANES - half, 1)
        down = pltpu.roll(t, half, 1)
        return t * cos + up * sa + down * sb

    def tile(c):
        return _dot(x, w_ref[:, c * LANES:(c + 1) * LANES])

    nq = q_ref.shape[1] // LANES
    nk = k_ref.shape[1] // LANES
    c = 0
    for j in range(nq):
        q_ref[:, j * LANES:(j + 1) * LANES] = rope(tile(c + j))
    c += nq
    for j in range(nk):
        k_ref[:, j * LANES:(j + 1) * LANES] = rope(tile(c + j))
    c += nk
    for j in range(nk):
        v_ref[:, j * LANES:(j + 1) * LANES] = tile(c + j)
    c += nk
    nqi = qi_ref.shape[1] // LANES
    for j in range(nqi):
        qi_ref[:, j * LANES:(j + 1) * LANES] = rope(tile(c + j))
    c += nqi
    ki_ref[...] = rope(tile(c))[:, :IDX_DIM]
    wi_ref[...] = tile(c + 1)
    dt_ref[...] = tile(c + 2)


def _proj_a(x, w_a, tables, tm):
    m, d = x.shape
    att_w = ATT_HEADS * ATT_HEAD_DIM
    kv_w = ATT_KV_HEADS * ATT_HEAD_DIM
    idx_w = IDX_HEADS * IDX_DIM
    tab_blocks = tables[0].shape[0] // tm
    row = lambda i: (i, 0)
    tab = lambda i: (i % tab_blocks, 0)
    full = lambda i: (0, 0)
    out_w = (att_w, kv_w, kv_w, idx_w, IDX_DIM, LANES, LANES)
    return pl.pallas_call(
        _proj_a_kernel,
        grid=(m // tm,),
        in_specs=[pl.BlockSpec((tm, d), row), pl.BlockSpec(w_a.shape, full)]
                 + [pl.BlockSpec((tm, LANES), tab)] * 3,
        out_specs=[pl.BlockSpec((tm, w), row) for w in out_w],
        out_shape=[jax.ShapeDtypeStruct((m, w), F32) for w in out_w],
        compiler_params=_cparams("parallel"),
        name="proj_a",
    )(x, w_a, *tables)


def _matmul_kernel(x_ref, w_ref, o_ref):
    o_ref[...] = _dot(x_ref[...].astype(BF16), w_ref[...])


def _matmul(x, w, tm, tn):
    m, k = x.shape
    n = w.shape[1]
    return pl.pallas_call(
        _matmul_kernel,
        grid=(m // tm, n // tn),
        in_specs=[pl.BlockSpec((tm, k), lambda i, j: (i, 0)), pl.BlockSpec((k, tn), lambda i, j: (0, j))],
        out_specs=pl.BlockSpec((tm, tn), lambda i, j: (i, j)),
        out_shape=jax.ShapeDtypeStruct((m, n), F32),
        compiler_params=_cparams("parallel", "arbitrary"),
        name="matmul",
    )(x, w)


def _lane_total(x01):
    return _dot(x01.astype(BF16), jnp.ones((LANES, LANES), BF16))


def _dsa_kernel(q_ref, qi_ref, wi_ref, k_ref, v_ref, ki_ref, o_ref,
                key_scr, sel_scr, lg_scr, m_scr, l_scr, acc_scr,
                *, qb, nkt, n_valid, q_pos0, k_sel):
    j = pl.program_id(1)
    q_first = q_pos0 + j * qb
    nt = jnp.minimum(nkt, (q_first + qb + LANES - 1) // LANES)

    rows = lax.broadcasted_iota(jnp.int32, (qb, LANES), 0)
    lanes = lax.broadcasted_iota(jnp.int32, (qb, LANES), 1)
    q_chunk = (q_first + rows) // CHUNK

    qi = qi_ref[...].astype(BF16)
    wi = wi_ref[...]
    wcols = [jnp.broadcast_to(wi[:, h:h + 1], (qb, LANES)) for h in range(IDX_HEADS)]

    def score_tile(kt, carry):
        ki_t = ki_ref[pl.ds(pl.multiple_of(kt * LANES, LANES), LANES), :].astype(BF16)
        s = jnp.zeros((qb, LANES), F32)
        for h in range(IDX_HEADS):
            lg = _dot_nt(qi[:, h * IDX_DIM:(h + 1) * IDX_DIM], ki_t)
            s = s + jnp.maximum(lg, 0.0) * wcols[h]
        kidx = kt * LANES + lanes
        adm = jnp.logical_and(kidx // CHUNK <= q_chunk, kidx < n_valid)
        s = jnp.where(s == 0.0, 0.0, s)
        s = jnp.where(adm, s, NEG_INF)
        bits = lax.bitcast_convert_type(s, jnp.int32)
        key_scr[kt] = jnp.where(bits < 0, bits ^ jnp.int32(0x7FFFFFFF), bits)
        return carry

    lax.fori_loop(0, nt, score_tile, 0)

    def count_ge(cand):
        def body(kt, acc):
            return acc + jnp.where(key_scr[kt] >= cand, 1.0, 0.0)
        return _lane_total(lax.fori_loop(0, nt, body, jnp.zeros((qb, LANES), F32)))

    kf = jnp.float32(k_sel)
    zero = jnp.zeros((qb, LANES), jnp.int32)
    thr = jnp.where(count_ge(zero) >= kf, zero, jnp.int32(INT32_MIN))

    def bit_pass(i, thr):
        cand = thr | jnp.left_shift(jnp.int32(1), 30 - i)
        return jnp.where(count_ge(cand) >= kf, cand, thr)

    thr = lax.fori_loop(0, 31, bit_pass, thr)

    def count_gt_body(kt, acc):
        return acc + jnp.where(key_scr[kt] > thr, 1.0, 0.0)

    n_gt = _lane_total(lax.fori_loop(0, nt, count_gt_body, jnp.zeros((qb, LANES), F32)))
    need = kf - n_gt
    ka = lax.broadcasted_iota(jnp.int32, (LANES, LANES), 0)
    kb = lax.broadcasted_iota(jnp.int32, (LANES, LANES), 1)
    before = jnp.where(ka < kb, 1.0, 0.0).astype(BF16)

    def select_tile(kt, run):
        key = key_scr[kt]
        eq = jnp.where(key == thr, 1.0, 0.0)
        rank = _dot(eq.astype(BF16), before) + run
        take = jnp.logical_or(key > thr, jnp.logical_and(key == thr, rank < need))
        kidx = kt * LANES + lanes
        adm = jnp.logical_and(kidx // CHUNK <= q_chunk, kidx < n_valid)
        sel_scr[kt] = jnp.where(jnp.logical_and(take, adm), 1.0, 0.0)
        return run + _lane_total(eq)

    lax.fori_loop(0, nt, select_tile, jnp.zeros((qb, LANES), F32))

    scale = ATT_HEAD_DIM ** -0.5
    hd = ATT_HEAD_DIM
    for g in range(ATT_KV_HEADS):
        qg = jnp.concatenate(
            [q_ref[:, (g * ATT_GROUP + r) * hd:(g * ATT_GROUP + r + 1) * hd] for r in range(ATT_GROUP)],
            axis=0)
        qg = (qg * scale).astype(BF16)
        m_scr[...] = jnp.full(m_scr.shape, NEG_INF, F32)

        def logits_tile(kt, carry):
            k_t = k_ref[pl.ds(pl.multiple_of(kt * LANES, LANES), LANES), g * hd:(g + 1) * hd].astype(BF16)
            lg = _dot_nt(qg, k_t)
            sel = sel_scr[kt]
            sel = jnp.concatenate([sel] * ATT_GROUP, axis=0)
            lg = jnp.where(sel > 0.0, lg, NEG_INF)
            lg_scr[kt] = lg
            m_scr[...] = jnp.maximum(m_scr[...], lg)
            return carry

        lax.fori_loop(0, nt, logits_tile, 0)
        m = jnp.max(m_scr[...], axis=1, keepdims=True)
        l_scr[...] = jnp.zeros(l_scr.shape, F32)
        acc_scr[...] = jnp.zeros(acc_scr.shape, F32)

        def pv_tile(kt, carry):
            v_t = v_ref[pl.ds(pl.multiple_of(kt * LANES, LANES), LANES), g * hd:(g + 1) * hd].astype(BF16)
            p = jnp.exp(lg_scr[kt] - m)
            l_scr[...] += p
            acc_scr[...] += _dot(p.astype(BF16), v_t)
            return carry

        lax.fori_loop(0, nt, pv_tile, 0)
        denom = jnp.sum(l_scr[...], axis=1, keepdims=True)
        out = acc_scr[...] / denom
        for r in range(ATT_GROUP):
            h = g * ATT_GROUP + r
            o_ref[:, h * hd:(h + 1) * hd] = out[r * qb:(r + 1) * qb]


def _dsa(q, qi, wi, k, v, ki, *, qb, n_valid, q_pos0, k_sel):
    b, lq, _ = q.shape
    nk = k.shape[1]
    nkt = nk // LANES
    qspec = lambda w: pl.BlockSpec((None, qb, w), lambda bi, j: (bi, j, 0))
    kspec = lambda w: pl.BlockSpec((None, nk, w), lambda bi, j: (bi, 0, 0))
    rows = ATT_GROUP * qb
    kern = functools.partial(_dsa_kernel, qb=qb, nkt=nkt, n_valid=n_valid, q_pos0=q_pos0, k_sel=k_sel)
    return pl.pallas_call(
        kern,
        grid=(b, lq // qb),
        in_specs=[qspec(q.shape[2]), qspec(qi.shape[2]), qspec(wi.shape[2]),
                  kspec(k.shape[2]), kspec(v.shape[2]), kspec(ki.shape[2])],
        out_specs=qspec(q.shape[2]),
        out_shape=jax.ShapeDtypeStruct(q.shape, F32),
        scratch_shapes=[
            pltpu.VMEM((nkt, qb, LANES), jnp.int32),
            pltpu.VMEM((nkt, qb, LANES), F32),
            pltpu.VMEM((nkt, rows, LANES), F32),
            pltpu.VMEM((rows, LANES), F32),
            pltpu.VMEM((rows, LANES), F32),
            pltpu.VMEM((rows, ATT_HEAD_DIM), F32),
        ],
        compiler_params=_cparams("parallel", "arbitrary"),
        name="dsa",
    )(q, qi, wi, k, v, ki)


def _silu(x):
    return x / (1.0 + jnp.exp(-x))


def _ssd_kernel(xbc_ref, z_ref, dt_ref, cprev_ref, sprev_ref, cw_ref, cb_ref, dtb_ref, alog_ref,
                dsk_ref, nw_ref, y_ref, cnew_ref, s_ref, buf_ref, yh_ref, *, l_valid):
    c = pl.program_id(1)
    lc = SSD_CHUNK
    hist = SUBLANES
    d_inner = y_ref.shape[1]
    n_heads = d_inner // SSM_HEAD_DIM
    gn = SSM_GROUPS * SSM_D_STATE
    heads_per_group = n_heads // SSM_GROUPS
    p = SSM_HEAD_DIM

    @pl.when(c == 0)
    def _():
        buf_ref[0:hist, :] = cprev_ref[...]
        s_ref[...] = sprev_ref[...]

    buf_ref[hist:hist + lc, :] = xbc_ref[...]
    conv = cb_ref[...] + sum(
        buf_ref[hist - (CONV_WIDTH - 1) + t:hist - (CONV_WIDTH - 1) + t + lc, :] * cw_ref[t:t + 1, :]
        for t in range(CONV_WIDTH))
    xbc = _silu(conv)
    cnew_ref[...] = buf_ref[l_valid:l_valid + hist, :]
    buf_ref[0:hist, :] = buf_ref[lc:lc + hist, :]

    xs = xbc[:, :d_inner]
    b_all = xbc[:, d_inner:d_inner + gn].astype(BF16)
    c_all = xbc[:, d_inner + gn:].astype(BF16)

    row = lax.broadcasted_iota(jnp.int32, (lc, LANES), 0)
    pre = dt_ref[...] + dtb_ref[...]
    dt = jnp.maximum(pre, 0.0) + jnp.log1p(jnp.exp(-jnp.abs(pre)))
    dt = jnp.where(row < l_valid, dt, 0.0)
    a = dt * (-jnp.exp(alog_ref[...]))
    ta = lax.broadcasted_iota(jnp.int32, (lc, lc), 0)
    tb = lax.broadcasted_iota(jnp.int32, (lc, lc), 1)
    tril = ta >= tb
    acum = jnp.dot(jnp.where(tril, 1.0, 0.0), a, preferred_element_type=F32,
                   precision=lax.Precision.HIGHEST)
    acum_t = acum.T
    dt_t = dt.T
    xs_t = xs.T

    for g in range(SSM_GROUPS):
        b_g = b_all[:, g * SSM_D_STATE:(g + 1) * SSM_D_STATE]
        c_g = c_all[:, g * SSM_D_STATE:(g + 1) * SSM_D_STATE]
        cb = _dot_nt(c_g, b_g)
        for hh in range(heads_per_group):
            h = g * heads_per_group + hh
            col = acum[:, h:h + 1]
            arow = acum_t[h:h + 1, :]
            dtrow = dt_t[h:h + 1, :]
            alast = acum_t[h:h + 1, lc - 1:lc]
            decay = jnp.where(tril, jnp.exp(jnp.where(tril, col - arow, 0.0)), 0.0)
            mh = (cb * decay * dtrow).astype(BF16)
            y_diag = _dot(mh, xs[:, h * p:(h + 1) * p].astype(BF16))
            s_h = s_ref[h * p:(h + 1) * p, :]
            y_off = _dot_nt(c_g, s_h.astype(BF16)) * jnp.exp(col)
            yh_ref[:, h * p:(h + 1) * p] = y_diag + y_off
            w_row = dtrow * jnp.exp(alast - arow)
            x_t = (xs_t[h * p:(h + 1) * p, :] * w_row).astype(BF16)
            s_ref[h * p:(h + 1) * p, :] = s_h * jnp.exp(alast) + _dot(x_t, b_g)

    y = yh_ref[...] + dsk_ref[...] * xs
    y = y * _silu(z_ref[...])
    gw = d_inner // SSM_GROUPS
    for g in range(SSM_GROUPS):
        yg = y[:, g * gw:(g + 1) * gw]
        ms = jnp.mean(yg * yg, axis=1, keepdims=True)
        y_ref[:, g * gw:(g + 1) * gw] = yg * lax.rsqrt(ms + RMS_EPS) * nw_ref[:, g * gw:(g + 1) * gw]


def _ssd(xbc, z_src, dt, conv_prev, ssm_prev, conv_w, conv_b, dt_bias, a_log, d_skip, norm_w, *, l_valid):
    b, l, conv_dim = xbc.shape
    d_inner = norm_w.shape[1]
    nc = l // SSD_CHUNK
    tok = lambda w: pl.BlockSpec((None, SSD_CHUNK, w), lambda bi, c: (bi, c, 0))
    per_b = lambda s: pl.BlockSpec((None,) + s, lambda bi, c: (bi, 0, 0))
    par = lambda a: pl.BlockSpec(a.shape, lambda bi, c: (0, 0))
    kern = functools.partial(_ssd_kernel, l_valid=l_valid)
    return pl.pallas_call(
        kern,
        grid=(b, nc),
        in_specs=[tok(conv_dim), tok(d_inner), tok(LANES),
                  per_b((SUBLANES, conv_dim)), per_b(ssm_prev.shape[1:]),
                  par(conv_w), par(conv_b), par(dt_bias), par(a_log), par(d_skip), par(norm_w)],
        out_specs=[tok(d_inner), per_b((SUBLANES, conv_dim)), per_b(ssm_prev.shape[1:])],
        out_shape=[jax.ShapeDtypeStruct((b, l, d_inner), F32),
                   jax.ShapeDtypeStruct((b, SUBLANES, conv_dim), F32),
                   jax.ShapeDtypeStruct(ssm_prev.shape, F32)],
        scratch_shapes=[pltpu.VMEM((SSD_CHUNK + 2 * SUBLANES, conv_dim), F32),
                        pltpu.VMEM((SSD_CHUNK, d_inner), F32)],
        compiler_params=_cparams("parallel", "arbitrary"),
        name="ssd",
    )(xbc, z_src, dt, conv_prev, ssm_prev, conv_w, conv_b, dt_bias, a_log, d_skip, norm_w)


def _layer_norm(x, g, b):
    mu = jnp.mean(x, axis=-1, keepdims=True)
    xc = x - mu
    var = jnp.mean(xc * xc, axis=-1, keepdims=True)
    return xc * lax.rsqrt(var + LN_EPS) * g + b


def _mix_kernel(x_ref, attn_ref, yssm_ref, gate_ref, bg_ref, wa_ref, wm_ref, wo_ref, g_ref, b_ref, o_ref,
                *, alpha):
    d = x_ref.shape[1]
    ya = _dot(attn_ref[...].astype(BF16), wa_ref[...])
    ym = _dot(yssm_ref[...].astype(BF16), wm_ref[...])
    gates = gate_ref[...] + bg_ref[...]
    gates = 1.0 / (1.0 + jnp.exp(-gates))
    mixed = gates[:, :d] * ya + gates[:, d:] * ym
    out = _dot(mixed.astype(BF16), wo_ref[...])
    o_ref[...] = _layer_norm(alpha * x_ref[...] + out, g_ref[...], b_ref[...])


def _mix(x, attn, y_ssm, zg, b_gate, w_a, w_m, w_o, ln_g, ln_b, *, alpha, tm):
    m, d = x.shape
    gate_block = zg.shape[1] // (N_BRANCHES * d) - 1
    row = lambda w: pl.BlockSpec((tm, w), lambda i: (i, 0))
    par = lambda a: pl.BlockSpec(a.shape, lambda i: (0, 0))
    return pl.pallas_call(
        functools.partial(_mix_kernel, alpha=alpha),
        grid=(m // tm,),
        in_specs=[row(d), row(attn.shape[1]), row(y_ssm.shape[1]),
                  pl.BlockSpec((tm, N_BRANCHES * d), lambda i: (i, gate_block)),
                  par(b_gate), par(w_a), par(w_m), par(w_o), par(ln_g), par(ln_b)],
        out_specs=row(d),
        out_shape=jax.ShapeDtypeStruct((m, d), F32),
        compiler_params=_cparams("parallel"),
        name="mix",
    )(x, attn, y_ssm, zg, b_gate, w_a, w_m, w_o, ln_g, ln_b)


def _top_values(x, n):
    vals = []
    for _ in range(n):
        m = jnp.max(x, axis=0, keepdims=True)
        vals.append(m)
        x = jnp.where(x == m, -jnp.inf, x)
    return vals


def _gelu(x):
    return 0.5 * x * (1.0 + lax.erf(x * (2.0 ** -0.5)))


def _peer_kernel(h_ref, wqt_ref, k1_ref, k2_ref, u_ref, vt_ref, g_ref, b_ref, o_ref,
                 ht_scr, s1_scr, s2_scr, e1_scr, e2_scr, tau_scr, acc_scr, *, alpha, i1_per_step):
    s = pl.program_id(1)
    half = PEER_QDIM // 2
    nk = PEER_NKEYS

    @pl.when(s == 0)
    def _():
        ht = h_ref[...].T.astype(BF16)
        ht_scr[...] = ht
        k1 = k1_ref[...].astype(BF16)
        k2 = k2_ref[...].astype(BF16)
        for hd in range(PEER_HEADS):
            qt = _dot(wqt_ref[hd * PEER_QDIM:(hd + 1) * PEER_QDIM, :], ht).astype(BF16)
            s1 = _dot(k1, qt[:half])
            s2 = _dot(k2, qt[half:])
            v1 = _top_values(s1, PEER_TOPK)
            v2 = _top_values(s2, PEER_TOPK)
            v2m = jnp.concatenate(v2, axis=0)
            cand = jnp.concatenate([v1[i] + v2m for i in range(PEER_TOPK)], axis=0)
            top = _top_values(cand, PEER_TOPK)
            zsum = sum(jnp.exp(t - top[0]) for t in top)
            s1_scr[hd] = s1
            s2_scr[hd] = s2
            e1_scr[hd] = jnp.exp(s1 - v1[0])
            e2_scr[hd] = jnp.exp(s2 - v2[0]) / zsum
            tau_scr[hd] = jnp.broadcast_to(top[PEER_TOPK - 1], tau_scr.shape[1:])
        acc_scr[...] = jnp.zeros(acc_scr.shape, F32)

    act = _gelu(_dot(u_ref[...], ht_scr[...]))
    pieces = []
    for ii in range(i1_per_step):
        i1 = s * i1_per_step + ii
        w = jnp.zeros((nk, act.shape[1]), F32)
        for hd in range(PEER_HEADS):
            r = s1_scr[hd, pl.ds(i1, 1), :]
            cw = e1_scr[hd, pl.ds(i1, 1), :]
            inside = (s2_scr[hd] + r) >= tau_scr[hd, 0:1, :]
            w = w + jnp.where(inside, e2_scr[hd] * cw, 0.0)
        pieces.append((w * act[ii * nk:(ii + 1) * nk]).astype(BF16))
    wa = jnp.concatenate(pieces, axis=0)
    acc_scr[...] += _dot(vt_ref[...], wa)

    @pl.when(s == pl.num_programs(1) - 1)
    def _():
        hblk = h_ref[...]
        o_ref[...] = _layer_norm(alpha * hblk + acc_scr[...].T, g_ref[...], b_ref[...])


def _peer(h, wq_t, keys1, keys2, u, v_t, ln_g, ln_b, *, alpha, tt, i1_per_step):
    m, d = h.shape
    n_exp = u.shape[0]
    eb = i1_per_step * PEER_NKEYS
    par = lambda a: pl.BlockSpec(a.shape, lambda i, s: (0, 0))
    scr = lambda: pltpu.VMEM((PEER_HEADS, PEER_NKEYS, tt), F32)
    return pl.pallas_call(
        functools.partial(_peer_kernel, alpha=alpha, i1_per_step=i1_per_step),
        grid=(m // tt, n_exp // eb),
        in_specs=[pl.BlockSpec((tt, d), lambda i, s: (i, 0)), par(wq_t), par(keys1), par(keys2),
                  pl.BlockSpec((eb, d), lambda i, s: (s, 0)), pl.BlockSpec((d, eb), lambda i, s: (0, s)),
                  par(ln_g), par(ln_b)],
        out_specs=pl.BlockSpec((tt, d), lambda i, s: (i, 0)),
        out_shape=jax.ShapeDtypeStruct((m, d), F32),
        scratch_shapes=[pltpu.VMEM((d, tt), BF16), scr(), scr(), scr(), scr(),
                        pltpu.VMEM((PEER_HEADS, SUBLANES, tt), F32),
                        pltpu.VMEM((d, tt), F32)],
        compiler_params=_cparams("parallel", "arbitrary"),
        name="peer",
    )(h, wq_t, keys1, keys2, u, v_t, ln_g, ln_b)


def _pad_cols(w, width):
    return jnp.pad(w, ((0, 0), (0, width - w.shape[1])))


def _prep_weights(w_in, d_model, d_inner, conv_dim, n_ssm_heads):
    att_w = ATT_HEADS * ATT_HEAD_DIM
    kv_w = ATT_KV_HEADS * ATT_HEAD_DIM
    splits = (att_w, kv_w, kv_w, IDX_HEADS * IDX_DIM, IDX_DIM, IDX_HEADS,
              d_inner, conv_dim, n_ssm_heads, N_BRANCHES * d_model)
    offs = np.cumsum((0,) + splits)
    piece = lambda i: w_in[:, offs[i]:offs[i + 1]]
    w_a = jnp.concatenate([piece(0), piece(1), piece(2), piece(3),
                           _pad_cols(piece(4), LANES), _pad_cols(piece(5), LANES), _pad_cols(piece(8), LANES)],
                          axis=1).astype(BF16)
    w_xbc = piece(7).astype(BF16)
    w_zg = jnp.concatenate([piece(6), piece(9)], axis=1).astype(BF16)
    return w_a, w_xbc, w_zg


def _stream(x, pos, past, wts, *, tm, qb, tt):
    (w_a, w_xbc, w_zg, b_gate, conv_w, conv_b, dt_bias, a_log, d_skip_row, norm_w, w_att, w_ssm, w_out,
     ln1_g, ln1_b, wq_t, keys1, keys2, u, v_t, ln2_g, ln2_b, alpha) = wts
    b, l, d = x.shape
    m = b * l
    xf = x.reshape(m, d)
    conv_dim = w_xbc.shape[1]
    d_inner = norm_w.shape[1]

    tables = _rope_tables(jnp.asarray(np.tile(pos, max(1, tm // l))))
    q, k, v, qi, ki, wi, dt = _proj_a(xf, w_a, tables, tm)
    xbc = _matmul(xf, w_xbc, tm, 1024)
    zg = _matmul(xf, w_zg, tm, 1024)

    r3 = lambda a: a.reshape(b, l, a.shape[1])
    if past is None:
        k_all, v_all, ki_all = r3(k), r3(v), r3(ki)
        n_valid = l
        k_sel = min(TOPK_MAX, l // 4)
        conv_prev = jnp.zeros((b, SUBLANES, conv_dim), F32)
        ssm_prev = jnp.zeros((b, d_inner, SSM_D_STATE), F32)
    else:
        ck, cv, cki, conv_state, ssm_state = past
        n_past = ck.shape[1]
        n_valid = n_past + l
        n_pad = -n_valid % LANES
        cat = lambda c, new: jnp.pad(jnp.concatenate([c.reshape(b, n_past, -1), r3(new)], axis=1),
                                     ((0, 0), (0, n_pad), (0, 0)))
        k_all, v_all, ki_all = cat(ck, k), cat(cv, v), cat(cki, ki)
        k_sel = min(TOPK_MAX, n_valid // 4)
        conv_prev = jnp.pad(conv_state, ((0, 0), (SUBLANES - (CONV_WIDTH - 1), 0), (0, 0)))
        ssm_prev = ssm_state.reshape(b, d_inner, SSM_D_STATE)
    attn = _dsa(r3(q), r3(qi), r3(wi), k_all, v_all, ki_all,
                qb=qb, n_valid=n_valid, q_pos0=int(pos[0]), k_sel=k_sel)

    l_pad = -l % SSD_CHUNK
    padl = lambda a: jnp.pad(r3(a), ((0, 0), (0, l_pad), (0, 0)))
    y_ssm, conv_new, ssm_new = _ssd(padl(xbc), padl(zg), padl(dt), conv_prev, ssm_prev,
                                    conv_w, conv_b, dt_bias, a_log, d_skip_row, norm_w,
                                    l_valid=SSD_CHUNK if l_pad == 0 else l)
    y_ssm = y_ssm[:, :l].reshape(m, d_inner)
    conv_new = conv_new[:, SUBLANES - (CONV_WIDTH - 1):]
    n_ssm_heads = d_inner // SSM_HEAD_DIM
    ssm_new = ssm_new.reshape(b, n_ssm_heads, SSM_HEAD_DIM, SSM_D_STATE)

    h1 = _mix(xf, attn.reshape(m, -1), y_ssm, zg, b_gate, w_att, w_ssm, w_out, ln1_g, ln1_b,
              alpha=alpha, tm=tm)
    y = _peer(h1, wq_t, keys1, keys2, u, v_t, ln2_g, ln2_b, alpha=alpha, tt=tt, i1_per_step=4)
    state = (r3(k).reshape(b, l, ATT_KV_HEADS, ATT_HEAD_DIM), r3(v).reshape(b, l, ATT_KV_HEADS, ATT_HEAD_DIM),
             r3(ki), conv_new, ssm_new)
    return y.reshape(b, l, d), state


def kernel(x_prompt, x_sample, cache_k, cache_v, cache_kidx, state_conv, state_ssm, w_in, b_gate, conv_w, conv_b, dt_bias, a_log, d_skip, ssm_norm_w, w_attn_br, w_ssm_br, w_out, ln1_g, ln1_b, peer_wq, peer_keys1, peer_keys2, peer_u, peer_v, ln2_g, ln2_b):
    depth = w_in.shape[0]
    d_model = x_prompt.shape[2]
    d_inner = ssm_norm_w.shape[1]
    conv_dim = conv_w.shape[2]
    n_ssm_heads = a_log.shape[1]
    alpha = (2.0 * depth) ** 0.25
    pos_p = np.arange(x_prompt.shape[1])
    pos_s = cache_k.shape[2] + np.arange(x_sample.shape[1])

    hp, hs = x_prompt, x_sample
    new_p, new_s = [], []
    for l in range(depth):
        w_a, w_xbc, w_zg = _prep_weights(w_in[l], d_model, d_inner, conv_dim, n_ssm_heads)
        row = lambda a: a.reshape(1, -1)
        wts = (w_a, w_xbc, w_zg, row(b_gate[l]),
               jnp.pad(conv_w[l], ((0, SUBLANES - CONV_WIDTH), (0, 0))), row(conv_b[l]),
               _pad_cols(row(dt_bias[l]), LANES), _pad_cols(row(a_log[l]), LANES),
               row(jnp.repeat(d_skip[l], SSM_HEAD_DIM)), row(ssm_norm_w[l]),
               w_attn_br[l].astype(BF16), w_ssm_br[l].astype(BF16), w_out[l].astype(BF16),
               row(ln1_g[l]), row(ln1_b[l]),
               peer_wq[l].T.astype(BF16), peer_keys1[l], peer_keys2[l],
               peer_u[l].astype(BF16), peer_v[l].T.astype(BF16), row(ln2_g[l]), row(ln2_b[l]), alpha)
        hp, sp = _stream(hp, pos_p, None, wts, tm=512, qb=128, tt=512)
        past = (cache_k[l], cache_v[l], cache_kidx[l], state_conv[l], state_ssm[l])
        hs, ss = _stream(hs, pos_s, past, wts, tm=256, qb=x_sample.shape[1], tt=256)
        new_p.append(sp)
        new_s.append(ss)

    stack = lambda lst, i: jnp.stack([e[i] for e in lst], axis=0)
    return (hp, hs) + tuple(stack(new_p, i) for i in range(5)) + tuple(stack(new_s, i) for i in range(5))
```

```python
import functools

import jax
import jax.numpy as jnp
import numpy as np
from jax import lax
from jax.experimental import pallas as pl
from jax.experimental.pallas import tpu as pltpu

F32 = jnp.float32
BF16 = jnp.bfloat16

LANES = 128
SUBLANES = 8
VMEM_LIMIT_BYTES = 48 * 1024 * 1024

CHUNK = 64
ATT_HEADS = 16
ATT_HEAD_DIM = 64
ATT_KV_HEADS = 4
ATT_GROUP = ATT_HEADS // ATT_KV_HEADS
IDX_HEADS = 4
IDX_DIM = 64
TOPK_MAX = 256
ROPE_THETA = 500000.0
ROPE_FRACTION = 4
SSM_HEAD_DIM = 64
SSM_GROUPS = 4
SSM_D_STATE = 128
CONV_WIDTH = 4
N_BRANCHES = 2
PEER_HEADS = 8
PEER_NKEYS = 128
PEER_QDIM = 256
PEER_TOPK = 16
LN_EPS = 1e-5
RMS_EPS = 1e-5
NEG_INF = -1e30
SSD_CHUNK = 128
INT32_MIN = -(2 ** 31)


def _cparams(*sem):
    return pltpu.CompilerParams(dimension_semantics=sem, vmem_limit_bytes=VMEM_LIMIT_BYTES)


def _dot(a, b):
    return jnp.dot(a, b, preferred_element_type=F32)


def _dot_nt(a, b):
    return lax.dot_general(a, b, (((1,), (1,)), ((), ())), preferred_element_type=F32)


def _rope_tables(pos):
    rd = ATT_HEAD_DIM // ROPE_FRACTION
    half = rd // 2
    inv = ROPE_THETA ** (-(jnp.arange(half, dtype=F32) * 2.0) / rd)
    ang = pos.astype(F32)[:, None] * inv[None, :]
    cos, sin = jnp.cos(ang), jnp.sin(ang)
    n = pos.shape[0]
    ones = jnp.ones((n, ATT_HEAD_DIM - rd), F32)
    zeros = jnp.zeros((n, ATT_HEAD_DIM - rd), F32)
    zh = jnp.zeros((n, half), F32)
    cos_h = jnp.concatenate([cos, cos, ones], axis=1)
    sa_h = jnp.concatenate([-sin, zh, zeros], axis=1)
    sb_h = jnp.concatenate([zh, sin, zeros], axis=1)
    rep = LANES // ATT_HEAD_DIM
    return jnp.tile(cos_h, (1, rep)), jnp.tile(sa_h, (1, rep)), jnp.tile(sb_h, (1, rep))


def _proj_a_kernel(x_ref, w_ref, cos_ref, sa_ref, sb_ref,
                   q_ref, k_ref, v_ref, qi_ref, ki_ref, wi_ref, dt_ref):
    x = x_ref[...].astype(BF16)
    cos, sa, sb = cos_ref[...], sa_ref[...], sb_ref[...]
    half = ATT_HEAD_DIM // ROPE_FRACTION // 2

    def rope(t):
        up = pltpu.roll(t, LANES - half, 1)
        down = pltpu.roll(t, half, 1)
        return t * cos + up * sa + down * sb

    def tile(c):
        return _dot(x, w_ref[:, c * LANES:(c + 1) * LANES])

    nq = q_ref.shape[1] // LANES
    nk = k_ref.shape[1] // LANES
    c = 0
    for j in range(nq):
        q_ref[:, j * LANES:(j + 1) * LANES] = rope(tile(c + j))
    c += nq
    for j in range(nk):
        k_ref[:, j * LANES:(j + 1) * LANES] = rope(tile(c + j))
    c += nk
    for j in range(nk):
        v_ref[:, j * LANES:(j + 1) * LANES] = tile(c + j)
    c += nk
    nqi = qi_ref.shape[1] // LANES
    for j in range(nqi):
        qi_ref[:, j * LANES:(j + 1) * LANES] = rope(tile(c + j))
    c += nqi
    ki_ref[...] = rope(tile(c))[:, :IDX_DIM]
    wi_ref[...] = tile(c + 1)
    dt_ref[...] = tile(c + 2)


def _proj_a(x, w_a, tables, tm):
    m, d = x.shape
    att_w = ATT_HEADS * ATT_HEAD_DIM
    kv_w = ATT_KV_HEADS * ATT_HEAD_DIM
    idx_w = IDX_HEADS * IDX_DIM
    tab_blocks = tables[0].shape[0] // tm
    row = lambda i: (i, 0)
    tab = lambda i: (i % tab_blocks, 0)
    full = lambda i: (0, 0)
    out_w = (att_w, kv_w, kv_w, idx_w, IDX_DIM, LANES, LANES)
    return pl.pallas_call(
        _proj_a_kernel,
        grid=(m // tm,),
        in_specs=[pl.BlockSpec((tm, d), row), pl.BlockSpec(w_a.shape, full)]
                 + [pl.BlockSpec((tm, LANES), tab)] * 3,
        out_specs=[pl.BlockSpec((tm, w), row) for w in out_w],
        out_shape=[jax.ShapeDtypeStruct((m, w), F32) for w in out_w],
        compiler_params=_cparams("parallel"),
        name="proj_a",
    )(x, w_a, *tables)


def _matmul_kernel(x_ref, w_ref, o_ref):
    o_ref[...] = _dot(x_ref[...].astype(BF16), w_ref[...])


def _matmul(x, w, tm, tn):
    m, k = x.shape
    n = w.shape[1]
    return pl.pallas_call(
        _matmul_kernel,
        grid=(m // tm, n // tn),
        in_specs=[pl.BlockSpec((tm, k), lambda i, j: (i, 0)), pl.BlockSpec((k, tn), lambda i, j: (0, j))],
        out_specs=pl.BlockSpec((tm, tn), lambda i, j: (i, j)),
        out_shape=jax.ShapeDtypeStruct((m, n), F32),
        compiler_params=_cparams("parallel", "arbitrary"),
        name="matmul",
    )(x, w)


def _dsa_kernel(q_ref, qi_ref, wi_ref, k_ref, v_ref, ki_ref, o_ref,
                key_scr, sel_scr, lg_scr, m_scr, l_scr, acc_scr,
                *, kt, nkt, n_valid, q_pos0, k_sel):
    qb = LANES
    j = pl.program_id(1)
    q_first = q_pos0 + j * qb
    nt = jnp.minimum(nkt, (q_first + qb + kt - 1) // kt)

    key_off = lax.broadcasted_iota(jnp.int32, (kt, qb), 0)
    q_chunk = (q_first + lax.broadcasted_iota(jnp.int32, (kt, qb), 1)) // CHUNK

    def admissible(t):
        kidx = t * kt + key_off
        return jnp.logical_and(kidx // CHUNK <= q_chunk, kidx < n_valid)

    qi_t = qi_ref[...].T.astype(BF16)
    wi_t = wi_ref[...].T

    def score_tile(t, carry):
        ki_t = ki_ref[pl.ds(pl.multiple_of(t * kt, LANES), kt), :].astype(BF16)
        s = jnp.zeros((kt, qb), F32)
        for h in range(IDX_HEADS):
            lg = _dot(ki_t, qi_t[h * IDX_DIM:(h + 1) * IDX_DIM, :])
            s = s + jnp.maximum(lg, 0.0) * wi_t[h:h + 1, :]
        s = jnp.where(s == 0.0, 0.0, s)
        s = jnp.where(admissible(t), s, NEG_INF)
        bits = lax.bitcast_convert_type(s, jnp.int32)
        key_scr[t] = jnp.where(bits < 0, bits ^ jnp.int32(0x7FFFFFFF), bits)
        return carry

    lax.fori_loop(0, nt, score_tile, 0)

    def count(pred):
        def body(t, acc):
            hit = jnp.where(pred(key_scr[t]), 1.0, 0.0)
            return acc + jnp.sum(hit.reshape(kt // SUBLANES, SUBLANES, qb), axis=0)
        acc = lax.fori_loop(0, nt, body, jnp.zeros((SUBLANES, qb), F32))
        return jnp.sum(acc, axis=0, keepdims=True)

    kf = jnp.float32(k_sel)
    zero = jnp.zeros((1, qb), jnp.int32)
    thr = jnp.where(count(lambda key: key >= zero) >= kf, zero, jnp.int32(INT32_MIN))

    def bit_pass(i, thr):
        cand = thr | jnp.left_shift(jnp.int32(1), 30 - i)
        return jnp.where(count(lambda key: key >= cand) >= kf, cand, thr)

    thr = lax.fori_loop(0, 31, bit_pass, thr)

    need = kf - count(lambda key: key > thr)
    ka = lax.broadcasted_iota(jnp.int32, (kt, kt), 0)
    kb = lax.broadcasted_iota(jnp.int32, (kt, kt), 1)
    earlier = jnp.where(kb < ka, 1.0, 0.0).astype(BF16)

    def select_tile(t, run):
        key = key_scr[t]
        tie = key == thr
        eq = jnp.where(tie, 1.0, 0.0)
        rank = _dot(earlier, eq.astype(BF16)) + run
        take = jnp.logical_or(key > thr, jnp.logical_and(tie, rank < need))
        sel = jnp.where(jnp.logical_and(take, admissible(t)), 1.0, 0.0)
        sel_scr[t] = sel.T
        return run + jnp.sum(eq, axis=0, keepdims=True)

    lax.fori_loop(0, nt, select_tile, jnp.zeros((1, qb), F32))

    scale = ATT_HEAD_DIM ** -0.5
    hd = ATT_HEAD_DIM
    rows = ATT_GROUP * qb
    for g in range(ATT_KV_HEADS):
        qg = jnp.concatenate(
            [q_ref[:, (g * ATT_GROUP + r) * hd:(g * ATT_GROUP + r + 1) * hd] for r in range(ATT_GROUP)],
            axis=0)
        qg = (qg * scale).astype(BF16)
        m_scr[...] = jnp.full(m_scr.shape, NEG_INF, F32)

        def logits_tile(t, carry):
            k_t = k_ref[pl.ds(pl.multiple_of(t * kt, LANES), kt), g * hd:(g + 1) * hd].astype(BF16)
            lg = _dot_nt(qg, k_t).reshape(ATT_GROUP, qb, kt)
            lg = jnp.where(sel_scr[t][None] > 0.0, lg, NEG_INF).reshape(rows, kt)
            lg_scr[t] = lg
            m_scr[...] = jnp.maximum(m_scr[...], functools.reduce(
                jnp.maximum, [lg[:, c * LANES:(c + 1) * LANES] for c in range(kt // LANES)]))
            return carry

        lax.fori_loop(0, nt, logits_tile, 0)
        m = jnp.max(m_scr[...], axis=1, keepdims=True)
        l_scr[...] = jnp.zeros(l_scr.shape, F32)
        acc_scr[...] = jnp.zeros(acc_scr.shape, F32)

        def pv_tile(t, carry):
            v_t = v_ref[pl.ds(pl.multiple_of(t * kt, LANES), kt), g * hd:(g + 1) * hd].astype(BF16)
            p = jnp.exp(lg_scr[t] - m)
            l_scr[...] += sum(p[:, c * LANES:(c + 1) * LANES] for c in range(kt // LANES))
            acc_scr[...] += _dot(p.astype(BF16), v_t)
            return carry

        lax.fori_loop(0, nt, pv_tile, 0)
        denom = jnp.sum(l_scr[...], axis=1, keepdims=True)
        out = acc_scr[...] / denom
        for r in range(ATT_GROUP):
            h = g * ATT_GROUP + r
            o_ref[:, h * hd:(h + 1) * hd] = out[r * qb:(r + 1) * qb]


def _dsa(q, qi, wi, k, v, ki, *, kt, n_valid, q_pos0, k_sel):
    b, lq, _ = q.shape
    nk = k.shape[1]
    nkt = nk // kt
    qb = LANES
    qspec = lambda w: pl.BlockSpec((None, qb, w), lambda bi, j: (bi, j, 0))
    kspec = lambda w: pl.BlockSpec((None, nk, w), lambda bi, j: (bi, 0, 0))
    rows = ATT_GROUP * qb
    kern = functools.partial(_dsa_kernel, kt=kt, nkt=nkt, n_valid=n_valid, q_pos0=q_pos0, k_sel=k_sel)
    return pl.pallas_call(
        kern,
        grid=(b, lq // qb),
        in_specs=[qspec(q.shape[2]), qspec(qi.shape[2]), qspec(wi.shape[2]),
                  kspec(k.shape[2]), kspec(v.shape[2]), kspec(ki.shape[2])],
        out_specs=qspec(q.shape[2]),
        out_shape=jax.ShapeDtypeStruct(q.shape, F32),
        scratch_shapes=[
            pltpu.VMEM((nkt, kt, qb), jnp.int32),
            pltpu.VMEM((nkt, qb, kt), F32),
            pltpu.VMEM((nkt, rows, kt), F32),
            pltpu.VMEM((rows, LANES), F32),
            pltpu.VMEM((rows, LANES), F32),
            pltpu.VMEM((rows, ATT_HEAD_DIM), F32),
        ],
        compiler_params=_cparams("parallel", "arbitrary"),
        name="dsa",
    )(q, qi, wi, k, v, ki)


def _silu(x):
    return x / (1.0 + jnp.exp(-x))


def _ssd_kernel(xbc_ref, z_ref, dt_ref, cprev_ref, sprev_ref, cw_ref, cb_ref, dtb_ref, alog_ref,
                dsk_ref, nw_ref, y_ref, cnew_ref, s_ref, buf_ref, yh_ref, *, l_valid):
    c = pl.program_id(1)
    lc = SSD_CHUNK
    hist = SUBLANES
    d_inner = y_ref.shape[1]
    n_heads = d_inner // SSM_HEAD_DIM
    gn = SSM_GROUPS * SSM_D_STATE
    heads_per_group = n_heads // SSM_GROUPS
    p = SSM_HEAD_DIM

    @pl.when(c == 0)
    def _():
        buf_ref[0:hist, :] = cprev_ref[...]
        s_ref[...] = sprev_ref[...]

    buf_ref[hist:hist + lc, :] = xbc_ref[...]
    conv = cb_ref[...] + sum(
        buf_ref[hist - (CONV_WIDTH - 1) + t:hist - (CONV_WIDTH - 1) + t + lc, :] * cw_ref[t:t + 1, :]
        for t in range(CONV_WIDTH))
    xbc = _silu(conv)
    cnew_ref[...] = buf_ref[l_valid:l_valid + hist, :]
    buf_ref[0:hist, :] = buf_ref[lc:lc + hist, :]

    xs = xbc[:, :d_inner]
    b_all = xbc[:, d_inner:d_inner + gn].astype(BF16)
    c_all = xbc[:, d_inner + gn:].astype(BF16)

    row = lax.broadcasted_iota(jnp.int32, (lc, LANES), 0)
    pre = dt_ref[...] + dtb_ref[...]
    dt = jnp.maximum(pre, 0.0) + jnp.log1p(jnp.exp(-jnp.abs(pre)))
    dt = jnp.where(row < l_valid, dt, 0.0)
    a = dt * (-jnp.exp(alog_ref[...]))
    ta = lax.broadcasted_iota(jnp.int32, (lc, lc), 0)
    tb = lax.broadcasted_iota(jnp.int32, (lc, lc), 1)
    tril = ta >= tb
    acum = jnp.dot(jnp.where(tril, 1.0, 0.0), a, preferred_element_type=F32,
                   precision=lax.Precision.HIGHEST)
    acum_t = acum.T
    dt_t = dt.T
    xs_t = xs.T

    for g in range(SSM_GROUPS):
        b_g = b_all[:, g * SSM_D_STATE:(g + 1) * SSM_D_STATE]
        c_g = c_all[:, g * SSM_D_STATE:(g + 1) * SSM_D_STATE]
        cb = _dot_nt(c_g, b_g)
        for hh in range(heads_per_group):
            h = g * heads_per_group + hh
            col = acum[:, h:h + 1]
            arow = acum_t[h:h + 1, :]
            dtrow = dt_t[h:h + 1, :]
            alast = acum_t[h:h + 1, lc - 1:lc]
            decay = jnp.where(tril, jnp.exp(jnp.where(tril, col - arow, 0.0)), 0.0)
            mh = (cb * decay * dtrow).astype(BF16)
            y_diag = _dot(mh, xs[:, h * p:(h + 1) * p].astype(BF16))
            s_h = s_ref[h * p:(h + 1) * p, :]
            y_off = _dot_nt(c_g, s_h.astype(BF16)) * jnp.exp(col)
            yh_ref[:, h * p:(h + 1) * p] = y_diag + y_off
            w_row = dtrow * jnp.exp(alast - arow)
            x_t = (xs_t[h * p:(h + 1) * p, :] * w_row).astype(BF16)
            s_ref[h * p:(h + 1) * p, :] = s_h * jnp.exp(alast) + _dot(x_t, b_g)

    y = yh_ref[...] + dsk_ref[...] * xs
    y = y * _silu(z_ref[...])
    gw = d_inner // SSM_GROUPS
    for g in range(SSM_GROUPS):
        yg = y[:, g * gw:(g + 1) * gw]
        ms = jnp.mean(yg * yg, axis=1, keepdims=True)
        y_ref[:, g * gw:(g + 1) * gw] = yg * lax.rsqrt(ms + RMS_EPS) * nw_ref[:, g * gw:(g + 1) * gw]


def _ssd(xbc, z_src, dt, conv_prev, ssm_prev, conv_w, conv_b, dt_bias, a_log, d_skip, norm_w, *, l_valid):
    b, l, conv_dim = xbc.shape
    d_inner = norm_w.shape[1]
    nc = l // SSD_CHUNK
    tok = lambda w: pl.BlockSpec((None, SSD_CHUNK, w), lambda bi, c: (bi, c, 0))
    per_b = lambda s: pl.BlockSpec((None,) + s, lambda bi, c: (bi, 0, 0))
    par = lambda a: pl.BlockSpec(a.shape, lambda bi, c: (0, 0))
    kern = functools.partial(_ssd_kernel, l_valid=l_valid)
    return pl.pallas_call(
        kern,
        grid=(b, nc),
        in_specs=[tok(conv_dim), tok(d_inner), tok(LANES),
                  per_b((SUBLANES, conv_dim)), per_b(ssm_prev.shape[1:]),
                  par(conv_w), par(conv_b), par(dt_bias), par(a_log), par(d_skip), par(norm_w)],
        out_specs=[tok(d_inner), per_b((SUBLANES, conv_dim)), per_b(ssm_prev.shape[1:])],
        out_shape=[jax.ShapeDtypeStruct((b, l, d_inner), F32),
                   jax.ShapeDtypeStruct((b, SUBLANES, conv_dim), F32),
                   jax.ShapeDtypeStruct(ssm_prev.shape, F32)],
        scratch_shapes=[pltpu.VMEM((SSD_CHUNK + 2 * SUBLANES, conv_dim), F32),
                        pltpu.VMEM((SSD_CHUNK, d_inner), F32)],
        compiler_params=_cparams("parallel", "arbitrary"),
        name="ssd",
    )(xbc, z_src, dt, conv_prev, ssm_prev, conv_w, conv_b, dt_bias, a_log, d_skip, norm_w)


def _layer_norm(x, g, b):
    mu = jnp.mean(x, axis=-1, keepdims=True)
    xc = x - mu
    var = jnp.mean(xc * xc, axis=-1, keepdims=True)
    return xc * lax.rsqrt(var + LN_EPS) * g + b


def _mix_kernel(x_ref, attn_ref, yssm_ref, gate_ref, bg_ref, wa_ref, wm_ref, wo_ref, g_ref, b_ref, o_ref,
                *, alpha):
    d = x_ref.shape[1]
    ya = _dot(attn_ref[...].astype(BF16), wa_ref[...])
    ym = _dot(yssm_ref[...].astype(BF16), wm_ref[...])
    gates = gate_ref[...] + bg_ref[...]
    gates = 1.0 / (1.0 + jnp.exp(-gates))
    mixed = gates[:, :d] * ya + gates[:, d:] * ym
    out = _dot(mixed.astype(BF16), wo_ref[...])
    o_ref[...] = _layer_norm(alpha * x_ref[...] + out, g_ref[...], b_ref[...])


def _mix(x, attn, y_ssm, zg, b_gate, w_a, w_m, w_o, ln_g, ln_b, *, alpha, tm):
    m, d = x.shape
    gate_block = zg.shape[1] // (N_BRANCHES * d) - 1
    row = lambda w: pl.BlockSpec((tm, w), lambda i: (i, 0))
    par = lambda a: pl.BlockSpec(a.shape, lambda i: (0, 0))
    return pl.pallas_call(
        functools.partial(_mix_kernel, alpha=alpha),
        grid=(m // tm,),
        in_specs=[row(d), row(attn.shape[1]), row(y_ssm.shape[1]),
                  pl.BlockSpec((tm, N_BRANCHES * d), lambda i: (i, gate_block)),
                  par(b_gate), par(w_a), par(w_m), par(w_o), par(ln_g), par(ln_b)],
        out_specs=row(d),
        out_shape=jax.ShapeDtypeStruct((m, d), F32),
        compiler_params=_cparams("parallel"),
        name="mix",
    )(x, attn, y_ssm, zg, b_gate, w_a, w_m, w_o, ln_g, ln_b)


def _top_values(x, n):
    vals = []
    for _ in range(n):
        m = jnp.max(x, axis=0, keepdims=True)
        vals.append(m)
        x = jnp.where(x == m, -jnp.inf, x)
    return vals


def _gelu(x):
    return 0.5 * x * (1.0 + lax.erf(x * (2.0 ** -0.5)))


def _peer_kernel(h_ref, wqt_ref, k1_ref, k2_ref, u_ref, vt_ref, g_ref, b_ref, o_ref,
                 ht_scr, th_scr, s2_scr, e1_scr, e2_scr, acc_scr, *, alpha, i1_per_step):
    s = pl.program_id(1)
    half = PEER_QDIM // 2
    nk = PEER_NKEYS

    @pl.when(s == 0)
    def _():
        ht = h_ref[...].T.astype(BF16)
        ht_scr[...] = ht
        k1 = k1_ref[...].astype(BF16)
        k2 = k2_ref[...].astype(BF16)
        for hd in range(PEER_HEADS):
            qt = _dot(wqt_ref[hd * PEER_QDIM:(hd + 1) * PEER_QDIM, :], ht).astype(BF16)
            s1 = _dot(k1, qt[:half])
            s2 = _dot(k2, qt[half:])
            v1 = _top_values(s1, PEER_TOPK)
            v2 = _top_values(s2, PEER_TOPK)
            cand = jnp.concatenate(
                [v1[i] + v2[jj] for i in range(PEER_TOPK) for jj in range(PEER_TOPK // (i + 1))], axis=0)
            top = _top_values(cand, PEER_TOPK)
            tau = top[PEER_TOPK - 1]
            zsum = sum(jnp.exp(t - top[0]) for t in top)
            theta = jnp.full(s1.shape, jnp.inf, F32)
            for jj in range(PEER_TOPK):
                theta = jnp.minimum(theta, jnp.where(s1 + v2[jj] >= tau, v2[jj], jnp.inf))
            th_scr[hd] = theta
            s2_scr[hd] = s2
            e1_scr[hd] = jnp.exp(s1 - v1[0])
            e2_scr[hd] = jnp.exp(s2 - v2[0]) / zsum
        acc_scr[...] = jnp.zeros(acc_scr.shape, F32)

    act = _gelu(_dot(u_ref[...], ht_scr[...]))
    pieces = []
    for ii in range(i1_per_step):
        i1 = s * i1_per_step + ii
        w = jnp.zeros((nk, act.shape[1]), F32)
        for hd in range(PEER_HEADS):
            th = th_scr[hd, pl.ds(i1, 1), :]
            cw = e1_scr[hd, pl.ds(i1, 1), :]
            w = w + jnp.where(s2_scr[hd] >= th, e2_scr[hd] * cw, 0.0)
        pieces.append((w * act[ii * nk:(ii + 1) * nk]).astype(BF16))
    wa = jnp.concatenate(pieces, axis=0)
    acc_scr[...] += _dot(vt_ref[...], wa)

    @pl.when(s == pl.num_programs(1) - 1)
    def _():
        hblk = h_ref[...]
        o_ref[...] = _layer_norm(alpha * hblk + acc_scr[...].T, g_ref[...], b_ref[...])


def _peer(h, wq_t, keys1, keys2, u, v_t, ln_g, ln_b, *, alpha, tt, i1_per_step):
    m, d = h.shape
    n_exp = u.shape[0]
    eb = i1_per_step * PEER_NKEYS
    par = lambda a: pl.BlockSpec(a.shape, lambda i, s: (0, 0))
    scr = lambda: pltpu.VMEM((PEER_HEADS, PEER_NKEYS, tt), F32)
    return pl.pallas_call(
        functools.partial(_peer_kernel, alpha=alpha, i1_per_step=i1_per_step),
        grid=(m // tt, n_exp // eb),
        in_specs=[pl.BlockSpec((tt, d), lambda i, s: (i, 0)), par(wq_t), par(keys1), par(keys2),
                  pl.BlockSpec((eb, d), lambda i, s: (s, 0)), pl.BlockSpec((d, eb), lambda i, s: (0, s)),
                  par(ln_g), par(ln_b)],
        out_specs=pl.BlockSpec((tt, d), lambda i, s: (i, 0)),
        out_shape=jax.ShapeDtypeStruct((m, d), F32),
        scratch_shapes=[pltpu.VMEM((d, tt), BF16), scr(), scr(), scr(), scr(),
                        pltpu.VMEM((d, tt), F32)],
        compiler_params=_cparams("parallel", "arbitrary"),
        name="peer",
    )(h, wq_t, keys1, keys2, u, v_t, ln_g, ln_b)


def _pad_cols(w, width):
    return jnp.pad(w, ((0, 0), (0, width - w.shape[1])))


def _prep_weights(w_in, d_model, d_inner, conv_dim, n_ssm_heads):
    att_w = ATT_HEADS * ATT_HEAD_DIM
    kv_w = ATT_KV_HEADS * ATT_HEAD_DIM
    splits = (att_w, kv_w, kv_w, IDX_HEADS * IDX_DIM, IDX_DIM, IDX_HEADS,
              d_inner, conv_dim, n_ssm_heads, N_BRANCHES * d_model)
    offs = np.cumsum((0,) + splits)
    piece = lambda i: w_in[:, offs[i]:offs[i + 1]]
    w_a = jnp.concatenate([piece(0), piece(1), piece(2), piece(3),
                           _pad_cols(piece(4), LANES), _pad_cols(piece(5), LANES), _pad_cols(piece(8), LANES)],
                          axis=1).astype(BF16)
    w_xbc = piece(7).astype(BF16)
    w_zg = jnp.concatenate([piece(6), piece(9)], axis=1).astype(BF16)
    return w_a, w_xbc, w_zg


def _stream(x, pos, past, wts, *, tm, kt, tt):
    (w_a, w_xbc, w_zg, b_gate, conv_w, conv_b, dt_bias, a_log, d_skip_row, norm_w, w_att, w_ssm, w_out,
     ln1_g, ln1_b, wq_t, keys1, keys2, u, v_t, ln2_g, ln2_b, alpha) = wts
    b, l, d = x.shape
    m = b * l
    xf = x.reshape(m, d)
    conv_dim = w_xbc.shape[1]
    d_inner = norm_w.shape[1]

    tables = _rope_tables(jnp.asarray(np.tile(pos, max(1, tm // l))))
    q, k, v, qi, ki, wi, dt = _proj_a(xf, w_a, tables, tm)
    xbc = _matmul(xf, w_xbc, tm, 1024)
    zg = _matmul(xf, w_zg, tm, 1024)

    r3 = lambda a: a.reshape(b, l, a.shape[1])
    if past is None:
        k_all, v_all, ki_all = r3(k), r3(v), r3(ki)
        n_valid = l
        k_sel = min(TOPK_MAX, l // 4)
        conv_prev = jnp.zeros((b, SUBLANES, conv_dim), F32)
        ssm_prev = jnp.zeros((b, d_inner, SSM_D_STATE), F32)
    else:
        ck, cv, cki, conv_state, ssm_state = past
        n_past = ck.shape[1]
        n_valid = n_past + l
        n_pad = -n_valid % kt
        cat = lambda c, new: jnp.pad(jnp.concatenate([c.reshape(b, n_past, -1), r3(new)], axis=1),
                                     ((0, 0), (0, n_pad), (0, 0)))
        k_all, v_all, ki_all = cat(ck, k), cat(cv, v), cat(cki, ki)
        k_sel = min(TOPK_MAX, n_valid // 4)
        conv_prev = jnp.pad(conv_state, ((0, 0), (SUBLANES - (CONV_WIDTH - 1), 0), (0, 0)))
        ssm_prev = ssm_state.reshape(b, d_inner, SSM_D_STATE)
    q_pad = -l % LANES
    padq = lambda a: jnp.pad(r3(a), ((0, 0), (0, q_pad), (0, 0)))
    attn = _dsa(padq(q), padq(qi), padq(wi), k_all, v_all, ki_all,
                kt=kt, n_valid=n_valid, q_pos0=int(pos[0]), k_sel=k_sel)[:, :l]

    l_pad = -l % SSD_CHUNK
    padl = lambda a: jnp.pad(r3(a), ((0, 0), (0, l_pad), (0, 0)))
    y_ssm, conv_new, ssm_new = _ssd(padl(xbc), padl(zg), padl(dt), conv_prev, ssm_prev,
                                    conv_w, conv_b, dt_bias, a_log, d_skip_row, norm_w,
                                    l_valid=SSD_CHUNK if l_pad == 0 else l)
    y_ssm = y_ssm[:, :l].reshape(m, d_inner)
    conv_new = conv_new[:, SUBLANES - (CONV_WIDTH - 1):]
    n_ssm_heads = d_inner // SSM_HEAD_DIM
    ssm_new = ssm_new.reshape(b, n_ssm_heads, SSM_HEAD_DIM, SSM_D_STATE)

    h1 = _mix(xf, attn.reshape(m, -1), y_ssm, zg, b_gate, w_att, w_ssm, w_out, ln1_g, ln1_b,
              alpha=alpha, tm=tm)
    y = _peer(h1, wq_t, keys1, keys2, u, v_t, ln2_g, ln2_b, alpha=alpha, tt=tt, i1_per_step=4)
    state = (r3(k).reshape(b, l, ATT_KV_HEADS, ATT_HEAD_DIM), r3(v).reshape(b, l, ATT_KV_HEADS, ATT_HEAD_DIM),
             r3(ki), conv_new, ssm_new)
    return y.reshape(b, l, d), state


def kernel(x_prompt, x_sample, cache_k, cache_v, cache_kidx, state_conv, state_ssm, w_in, b_gate, conv_w, conv_b, dt_bias, a_log, d_skip, ssm_norm_w, w_attn_br, w_ssm_br, w_out, ln1_g, ln1_b, peer_wq, peer_keys1, peer_keys2, peer_u, peer_v, ln2_g, ln2_b):
    depth = w_in.shape[0]
    d_model = x_prompt.shape[2]
    d_inner = ssm_norm_w.shape[1]
    conv_dim = conv_w.shape[2]
    n_ssm_heads = a_log.shape[1]
    alpha = (2.0 * depth) ** 0.25
    pos_p = np.arange(x_prompt.shape[1])
    pos_s = cache_k.shape[2] + np.arange(x_sample.shape[1])

    hp, hs = x_prompt, x_sample
    new_p, new_s = [], []
    for l in range(depth):
        w_a, w_xbc, w_zg = _prep_weights(w_in[l], d_model, d_inner, conv_dim, n_ssm_heads)
        row = lambda a: a.reshape(1, -1)
        wts = (w_a, w_xbc, w_zg, row(b_gate[l]),
               jnp.pad(conv_w[l], ((0, SUBLANES - CONV_WIDTH), (0, 0))), row(conv_b[l]),
               _pad_cols(row(dt_bias[l]), LANES), _pad_cols(row(a_log[l]), LANES),
               row(jnp.repeat(d_skip[l], SSM_HEAD_DIM)), row(ssm_norm_w[l]),
               w_attn_br[l].astype(BF16), w_ssm_br[l].astype(BF16), w_out[l].astype(BF16),
               row(ln1_g[l]), row(ln1_b[l]),
               peer_wq[l].T.astype(BF16), peer_keys1[l], peer_keys2[l],
               peer_u[l].astype(BF16), peer_v[l].T.astype(BF16), row(ln2_g[l]), row(ln2_b[l]), alpha)
        hp, sp = _stream(hp, pos_p, None, wts, tm=512, kt=512, tt=512)
        past = (cache_k[l], cache_v[l], cache_kidx[l], state_conv[l], state_ssm[l])
        hs, ss = _stream(hs, pos_s, past, wts, tm=256, kt=384, tt=256)
        new_p.append(sp)
        new_s.append(ss)

    stack = lambda lst, i: jnp.stack([e[i] for e in lst], axis=0)
    return (hp, hs) + tuple(stack(new_p, i) for i in range(5)) + tuple(stack(new_s, i) for i in range(5))
```

```python
import functools

import jax
import jax.numpy as jnp
import numpy as np
from jax import lax
from jax.experimental import pallas as pl
from jax.experimental.pallas import tpu as pltpu

F32 = jnp.float32
BF16 = jnp.bfloat16

LANES = 128
SUBLANES = 8
VMEM_LIMIT_BYTES = 48 * 1024 * 1024

CHUNK = 64
ATT_HEADS = 16
ATT_HEAD_DIM = 64
ATT_KV_HEADS = 4
ATT_GROUP = ATT_HEADS // ATT_KV_HEADS
IDX_HEADS = 4
IDX_DIM = 64
TOPK_MAX = 256
ROPE_THETA = 500000.0
ROPE_FRACTION = 4
SSM_HEAD_DIM = 64
SSM_GROUPS = 4
SSM_D_STATE = 128
CONV_WIDTH = 4
N_BRANCHES = 2
PEER_HEADS = 8
PEER_NKEYS = 128
PEER_QDIM = 256
PEER_TOPK = 16
LN_EPS = 1e-5
RMS_EPS = 1e-5
NEG_INF = -1e30
SSD_CHUNK = 128
INT32_MIN = -(2 ** 31)


def _cparams(*sem):
    return pltpu.CompilerParams(dimension_semantics=sem, vmem_limit_bytes=VMEM_LIMIT_BYTES)


def _dot(a, b):
    return jnp.dot(a, b, preferred_element_type=F32)


def _dot_nt(a, b):
    return lax.dot_general(a, b, (((1,), (1,)), ((), ())), preferred_element_type=F32)


def _rope_tables(pos):
    rd = ATT_HEAD_DIM // ROPE_FRACTION
    half = rd // 2
    inv = ROPE_THETA ** (-(jnp.arange(half, dtype=F32) * 2.0) / rd)
    ang = pos.astype(F32)[:, None] * inv[None, :]
    cos, sin = jnp.cos(ang), jnp.sin(ang)
    n = pos.shape[0]
    ones = jnp.ones((n, ATT_HEAD_DIM - rd), F32)
    zeros = jnp.zeros((n, ATT_HEAD_DIM - rd), F32)
    zh = jnp.zeros((n, half), F32)
    cos_h = jnp.concatenate([cos, cos, ones], axis=1)
    sa_h = jnp.concatenate([-sin, zh, zeros], axis=1)
    sb_h = jnp.concatenate([zh, sin, zeros], axis=1)
    rep = LANES // ATT_HEAD_DIM
    return jnp.tile(cos_h, (1, rep)), jnp.tile(sa_h, (1, rep)), jnp.tile(sb_h, (1, rep))


def _proj_a_kernel(x_ref, w_ref, cos_ref, sa_ref, sb_ref,
                   q_ref, k_ref, v_ref, qi_ref, ki_ref, wi_ref, dt_ref):
    x = x_ref[...].astype(BF16)
    cos, sa, sb = cos_ref[...], sa_ref[...], sb_ref[...]
    half = ATT_HEAD_DIM // ROPE_FRACTION // 2

    def rope(t):
        up = pltpu.roll(t, LANES - half, 1)
        down = pltpu.roll(t, half, 1)
        return t * cos + up * sa + down * sb

    def tile(c):
        return _dot(x, w_ref[:, c * LANES:(c + 1) * LANES])

    nq = q_ref.shape[1] // LANES
    nk = k_ref.shape[1] // LANES
    c = 0
    for j in range(nq):
        q_ref[:, j * LANES:(j + 1) * LANES] = rope(tile(c + j))
    c += nq
    for j in range(nk):
        k_ref[:, j * LANES:(j + 1) * LANES] = rope(tile(c + j))
    c += nk
    for j in range(nk):
        v_ref[:, j * LANES:(j + 1) * LANES] = tile(c + j)
    c += nk
    nqi = qi_ref.shape[1] // LANES
    for j in range(nqi):
        qi_ref[:, j * LANES:(j + 1) * LANES] = rope(tile(c + j))
    c += nqi
    ki_ref[...] = rope(tile(c))[:, :IDX_DIM]
    wi_ref[...] = tile(c + 1)
    dt_ref[...] = tile(c + 2)


def _proj_a(x, w_a, tables, tm):
    m, d = x.shape
    att_w = ATT_HEADS * ATT_HEAD_DIM
    kv_w = ATT_KV_HEADS * ATT_HEAD_DIM
    idx_w = IDX_HEADS * IDX_DIM
    tab_blocks = tables[0].shape[0] // tm
    row = lambda i: (i, 0)
    tab = lambda i: (i % tab_blocks, 0)
    full = lambda i: (0, 0)
    out_w = (att_w, kv_w, kv_w, idx_w, IDX_DIM, LANES, LANES)
    return pl.pallas_call(
        _proj_a_kernel,
        grid=(m // tm,),
        in_specs=[pl.BlockSpec((tm, d), row), pl.BlockSpec(w_a.shape, full)]
                 + [pl.BlockSpec((tm, LANES), tab)] * 3,
        out_specs=[pl.BlockSpec((tm, w), row) for w in out_w],
        out_shape=[jax.ShapeDtypeStruct((m, w), F32) for w in out_w],
        compiler_params=_cparams("parallel"),
        name="proj_a",
    )(x, w_a, *tables)


def _matmul_kernel(x_ref, w_ref, o_ref):
    o_ref[...] = _dot(x_ref[...].astype(BF16), w_ref[...])


def _matmul(x, w, tm, tn):
    m, k = x.shape
    n = w.shape[1]
    return pl.pallas_call(
        _matmul_kernel,
        grid=(m // tm, n // tn),
        in_specs=[pl.BlockSpec((tm, k), lambda i, j: (i, 0)), pl.BlockSpec((k, tn), lambda i, j: (0, j))],
        out_specs=pl.BlockSpec((tm, tn), lambda i, j: (i, j)),
        out_shape=jax.ShapeDtypeStruct((m, n), F32),
        compiler_params=_cparams("parallel", "arbitrary"),
        name="matmul",
    )(x, w)


def _tree(op, xs):
    xs = list(xs)
    while len(xs) > 1:
        xs = [op(xs[i], xs[i + 1]) for i in range(0, len(xs) - 1, 2)] + ([xs[-1]] if len(xs) % 2 else [])
    return xs[0]


def _lane_tiles(x):
    return [x[:, c * LANES:(c + 1) * LANES] for c in range(x.shape[1] // LANES)]


def _dsa_kernel(q_ref, qi_ref, wi_ref, k_ref, v_ref, ki_ref, o_ref,
                key_scr, sel_scr, lg_scr, q_scr, m_scr, l_scr, acc_scr,
                *, kt, nkt, n_valid, q_pos0, k_sel):
    qb = LANES
    j = pl.program_id(1)
    q_first = q_pos0 + j * qb
    nt = jnp.minimum(nkt, (q_first + qb + kt - 1) // kt)

    key_off = lax.broadcasted_iota(jnp.int32, (kt, qb), 0)
    q_chunk = (q_first + lax.broadcasted_iota(jnp.int32, (1, qb), 1)) // CHUNK
    key_limit = jnp.minimum((q_chunk + 1) * CHUNK, n_valid)

    def admissible(t):
        return key_off < key_limit - t * kt

    qi_t = qi_ref[...].T.astype(BF16)
    wi_t = wi_ref[...].T

    def score_tile(t, carry):
        ki_t = ki_ref[pl.ds(pl.multiple_of(t * kt, LANES), kt), :].astype(BF16)
        s = jnp.zeros((kt, qb), F32)
        for h in range(IDX_HEADS):
            lg = _dot(ki_t, qi_t[h * IDX_DIM:(h + 1) * IDX_DIM, :])
            s = s + jnp.maximum(lg, 0.0) * wi_t[h:h + 1, :]
        s = jnp.where(s == 0.0, 0.0, s)
        s = jnp.where(admissible(t), s, NEG_INF)
        bits = lax.bitcast_convert_type(s, jnp.int32)
        key_scr[t] = jnp.where(bits < 0, bits ^ jnp.int32(0x7FFFFFFF), bits)
        return carry

    lax.fori_loop(0, nt, score_tile, 0)

    def count(pred):
        def body(t, acc):
            hit = jnp.where(pred(key_scr[t]), 1.0, 0.0)
            return acc + _tree(jnp.add, [hit[r:r + SUBLANES] for r in range(0, kt, SUBLANES)])
        acc = lax.fori_loop(0, nt, body, jnp.zeros((SUBLANES, qb), F32))
        return jnp.sum(acc, axis=0, keepdims=True)

    kf = jnp.float32(k_sel)
    zero = jnp.zeros((1, qb), jnp.int32)
    thr = jnp.where(count(lambda key: key >= zero) >= kf, zero, jnp.int32(INT32_MIN))

    def bit_pass(i, thr):
        cand = thr | jnp.left_shift(jnp.int32(1), 30 - i)
        return jnp.where(count(lambda key: key >= cand) >= kf, cand, thr)

    thr = lax.fori_loop(0, 31, bit_pass, thr)

    need = kf - count(lambda key: key > thr)
    ka = lax.broadcasted_iota(jnp.int32, (kt, kt), 0)
    kb = lax.broadcasted_iota(jnp.int32, (kt, kt), 1)
    earlier = jnp.where(kb < ka, 1.0, 0.0).astype(BF16)

    def select_tile(t, run):
        key = key_scr[t]
        tie = key == thr
        eq = jnp.where(tie, 1.0, 0.0)
        rank = _dot(earlier, eq.astype(BF16)) + run
        take = jnp.logical_or(key > thr, jnp.logical_and(tie, rank < need))
        sel = jnp.where(jnp.logical_and(take, admissible(t)), 1.0, 0.0)
        sel_scr[t] = sel.T
        return run + jnp.sum(eq, axis=0, keepdims=True)

    lax.fori_loop(0, nt, select_tile, jnp.zeros((1, qb), F32))

    scale = ATT_HEAD_DIM ** -0.5
    hd = ATT_HEAD_DIM
    rows = ATT_GROUP * qb
    pair = LANES // hd
    for g in range(ATT_KV_HEADS):
        qg = jnp.concatenate(
            [q_ref[:, (g * ATT_GROUP + r) * hd:(g * ATT_GROUP + r + 1) * hd] for r in range(ATT_GROUP)],
            axis=0) * scale
        zeros = jnp.zeros_like(qg)
        parts = [qg if c == g % pair else zeros for c in range(pair)]
        q_scr[g] = jnp.concatenate(parts, axis=1).astype(BF16)
    m_scr[...] = jnp.full(m_scr.shape, NEG_INF, F32)
    l_scr[...] = jnp.zeros(l_scr.shape, F32)
    acc_scr[...] = jnp.zeros(acc_scr.shape, F32)

    def kv_tile(ref, t, g):
        lane0 = (g // pair) * LANES
        return ref[pl.ds(pl.multiple_of(t * kt, LANES), kt), lane0:lane0 + LANES].astype(BF16)

    def logits_tile(t, carry):
        sel = sel_scr[t][None] > 0.0
        for g in range(ATT_KV_HEADS):
            lg = _dot_nt(q_scr[g], kv_tile(k_ref, t, g)).reshape(ATT_GROUP, qb, kt)
            lg = jnp.where(sel, lg, NEG_INF).reshape(rows, kt)
            lg_scr[t, g] = lg
            m_scr[g] = jnp.maximum(m_scr[g], _tree(jnp.maximum, _lane_tiles(lg)))
        return carry

    lax.fori_loop(0, nt, logits_tile, 0)
    for g in range(ATT_KV_HEADS):
        m_scr[g] = jnp.broadcast_to(jnp.max(m_scr[g], axis=1, keepdims=True), (rows, LANES))

    def pv_tile(t, carry):
        for g in range(ATT_KV_HEADS):
            m = m_scr[g]
            ps = [jnp.exp(lg_c - m) for lg_c in _lane_tiles(lg_scr[t, g])]
            l_scr[g] += _tree(jnp.add, ps)
            acc_scr[g] += _dot(jnp.concatenate(ps, axis=1).astype(BF16), kv_tile(v_ref, t, g))
        return carry

    lax.fori_loop(0, nt, pv_tile, 0)
    for g in range(ATT_KV_HEADS):
        denom = jnp.sum(l_scr[g], axis=1, keepdims=True)
        out = acc_scr[g][:, (g % pair) * hd:(g % pair + 1) * hd] / denom
        for r in range(ATT_GROUP):
            h = g * ATT_GROUP + r
            o_ref[:, h * hd:(h + 1) * hd] = out[r * qb:(r + 1) * qb]


def _dsa(q, qi, wi, k, v, ki, *, kt, n_valid, q_pos0, k_sel):
    b, lq, _ = q.shape
    nk = k.shape[1]
    nkt = nk // kt
    qb = LANES
    qspec = lambda w: pl.BlockSpec((None, qb, w), lambda bi, j: (bi, j, 0))
    kspec = lambda w: pl.BlockSpec((None, nk, w), lambda bi, j: (bi, 0, 0))
    rows = ATT_GROUP * qb
    kern = functools.partial(_dsa_kernel, kt=kt, nkt=nkt, n_valid=n_valid, q_pos0=q_pos0, k_sel=k_sel)
    return pl.pallas_call(
        kern,
        grid=(b, lq // qb),
        in_specs=[qspec(q.shape[2]), qspec(qi.shape[2]), qspec(wi.shape[2]),
                  kspec(k.shape[2]), kspec(v.shape[2]), kspec(ki.shape[2])],
        out_specs=qspec(q.shape[2]),
        out_shape=jax.ShapeDtypeStruct(q.shape, F32),
        scratch_shapes=[
            pltpu.VMEM((nkt, kt, qb), jnp.int32),
            pltpu.VMEM((nkt, qb, kt), F32),
            pltpu.VMEM((nkt, ATT_KV_HEADS, rows, kt), F32),
            pltpu.VMEM((ATT_KV_HEADS, rows, LANES), BF16),
            pltpu.VMEM((ATT_KV_HEADS, rows, LANES), F32),
            pltpu.VMEM((ATT_KV_HEADS, rows, LANES), F32),
            pltpu.VMEM((ATT_KV_HEADS, rows, LANES), F32),
        ],
        compiler_params=_cparams("parallel", "arbitrary"),
        name="dsa",
    )(q, qi, wi, k, v, ki)


def _silu(x):
    return x / (1.0 + jnp.exp(-x))


def _ssd_kernel(xbc_ref, z_ref, dt_ref, cprev_ref, sprev_ref, cw_ref, cb_ref, dtb_ref, alog_ref,
                dsk_ref, nw_ref, y_ref, cnew_ref, s_ref, buf_ref, yh_ref, *, l_valid):
    c = pl.program_id(1)
    lc = SSD_CHUNK
    hist = SUBLANES
    d_inner = y_ref.shape[1]
    n_heads = d_inner // SSM_HEAD_DIM
    gn = SSM_GROUPS * SSM_D_STATE
    heads_per_group = n_heads // SSM_GROUPS
    p = SSM_HEAD_DIM

    @pl.when(c == 0)
    def _():
        buf_ref[0:hist, :] = cprev_ref[...]
        s_ref[...] = sprev_ref[...]

    buf_ref[hist:hist + lc, :] = xbc_ref[...]
    conv = cb_ref[...] + sum(
        buf_ref[hist - (CONV_WIDTH - 1) + t:hist - (CONV_WIDTH - 1) + t + lc, :] * cw_ref[t:t + 1, :]
        for t in range(CONV_WIDTH))
    xbc = _silu(conv)
    cnew_ref[...] = buf_ref[l_valid:l_valid + hist, :]
    buf_ref[0:hist, :] = buf_ref[lc:lc + hist, :]

    xs = xbc[:, :d_inner]
    b_all = xbc[:, d_inner:d_inner + gn].astype(BF16)
    c_all = xbc[:, d_inner + gn:].astype(BF16)

    row = lax.broadcasted_iota(jnp.int32, (lc, LANES), 0)
    pre = dt_ref[...] + dtb_ref[...]
    dt = jnp.maximum(pre, 0.0) + jnp.log1p(jnp.exp(-jnp.abs(pre)))
    dt = jnp.where(row < l_valid, dt, 0.0)
    a = dt * (-jnp.exp(alog_ref[...]))
    ta = lax.broadcasted_iota(jnp.int32, (lc, lc), 0)
    tb = lax.broadcasted_iota(jnp.int32, (lc, lc), 1)
    tril = ta >= tb
    acum = jnp.dot(jnp.where(tril, 1.0, 0.0), a, preferred_element_type=F32,
                   precision=lax.Precision.HIGHEST)
    acum_t = acum.T
    dt_t = dt.T
    xs_t = xs.T

    for g in range(SSM_GROUPS):
        b_g = b_all[:, g * SSM_D_STATE:(g + 1) * SSM_D_STATE]
        c_g = c_all[:, g * SSM_D_STATE:(g + 1) * SSM_D_STATE]
        cb = _dot_nt(c_g, b_g)
        for hh in range(heads_per_group):
            h = g * heads_per_group + hh
            col = acum[:, h:h + 1]
            arow = acum_t[h:h + 1, :]
            dtrow = dt_t[h:h + 1, :]
            alast = acum_t[h:h + 1, lc - 1:lc]
            decay = jnp.where(tril, jnp.exp(jnp.where(tril, col - arow, 0.0)), 0.0)
            mh = (cb * decay * dtrow).astype(BF16)
            y_diag = _dot(mh, xs[:, h * p:(h + 1) * p].astype(BF16))
            s_h = s_ref[h * p:(h + 1) * p, :]
            y_off = _dot_nt(c_g, s_h.astype(BF16)) * jnp.exp(col)
            yh_ref[:, h * p:(h + 1) * p] = y_diag + y_off
            w_row = dtrow * jnp.exp(alast - arow)
            x_t = (xs_t[h * p:(h + 1) * p, :] * w_row).astype(BF16)
            s_ref[h * p:(h + 1) * p, :] = s_h * jnp.exp(alast) + _dot(x_t, b_g)

    y = yh_ref[...] + dsk_ref[...] * xs
    y = y * _silu(z_ref[...])
    gw = d_inner // SSM_GROUPS
    for g in range(SSM_GROUPS):
        yg = y[:, g * gw:(g + 1) * gw]
        ms = jnp.mean(yg * yg, axis=1, keepdims=True)
        y_ref[:, g * gw:(g + 1) * gw] = yg * lax.rsqrt(ms + RMS_EPS) * nw_ref[:, g * gw:(g + 1) * gw]


def _ssd(xbc, z_src, dt, conv_prev, ssm_prev, conv_w, conv_b, dt_bias, a_log, d_skip, norm_w, *, l_valid):
    b, l, conv_dim = xbc.shape
    d_inner = norm_w.shape[1]
    nc = l // SSD_CHUNK
    tok = lambda w: pl.BlockSpec((None, SSD_CHUNK, w), lambda bi, c: (bi, c, 0))
    per_b = lambda s: pl.BlockSpec((None,) + s, lambda bi, c: (bi, 0, 0))
    par = lambda a: pl.BlockSpec(a.shape, lambda bi, c: (0, 0))
    kern = functools.partial(_ssd_kernel, l_valid=l_valid)
    return pl.pallas_call(
        kern,
        grid=(b, nc),
        in_specs=[tok(conv_dim), tok(d_inner), tok(LANES),
                  per_b((SUBLANES, conv_dim)), per_b(ssm_prev.shape[1:]),
                  par(conv_w), par(conv_b), par(dt_bias), par(a_log), par(d_skip), par(norm_w)],
        out_specs=[tok(d_inner), per_b((SUBLANES, conv_dim)), per_b(ssm_prev.shape[1:])],
        out_shape=[jax.ShapeDtypeStruct((b, l, d_inner), F32),
                   jax.ShapeDtypeStruct((b, SUBLANES, conv_dim), F32),
                   jax.ShapeDtypeStruct(ssm_prev.shape, F32)],
        scratch_shapes=[pltpu.VMEM((SSD_CHUNK + 2 * SUBLANES, conv_dim), F32),
                        pltpu.VMEM((SSD_CHUNK, d_inner), F32)],
        compiler_params=_cparams("parallel", "arbitrary"),
        name="ssd",
    )(xbc, z_src, dt, conv_prev, ssm_prev, conv_w, conv_b, dt_bias, a_log, d_skip, norm_w)


def _layer_norm(x, g, b):
    mu = jnp.mean(x, axis=-1, keepdims=True)
    xc = x - mu
    var = jnp.mean(xc * xc, axis=-1, keepdims=True)
    return xc * lax.rsqrt(var + LN_EPS) * g + b


def _mix_kernel(x_ref, attn_ref, yssm_ref, gate_ref, bg_ref, wa_ref, wm_ref, wo_ref, g_ref, b_ref, o_ref,
                *, alpha):
    d = x_ref.shape[1]
    ya = _dot(attn_ref[...].astype(BF16), wa_ref[...])
    ym = _dot(yssm_ref[...].astype(BF16), wm_ref[...])
    gates = gate_ref[...] + bg_ref[...]
    gates = 1.0 / (1.0 + jnp.exp(-gates))
    mixed = gates[:, :d] * ya + gates[:, d:] * ym
    out = _dot(mixed.astype(BF16), wo_ref[...])
    o_ref[...] = _layer_norm(alpha * x_ref[...] + out, g_ref[...], b_ref[...])


def _mix(x, attn, y_ssm, zg, b_gate, w_a, w_m, w_o, ln_g, ln_b, *, alpha, tm):
    m, d = x.shape
    gate_block = zg.shape[1] // (N_BRANCHES * d) - 1
    row = lambda w: pl.BlockSpec((tm, w), lambda i: (i, 0))
    par = lambda a: pl.BlockSpec(a.shape, lambda i: (0, 0))
    return pl.pallas_call(
        functools.partial(_mix_kernel, alpha=alpha),
        grid=(m // tm,),
        in_specs=[row(d), row(attn.shape[1]), row(y_ssm.shape[1]),
                  pl.BlockSpec((tm, N_BRANCHES * d), lambda i: (i, gate_block)),
                  par(b_gate), par(w_a), par(w_m), par(w_o), par(ln_g), par(ln_b)],
        out_specs=row(d),
        out_shape=jax.ShapeDtypeStruct((m, d), F32),
        compiler_params=_cparams("parallel"),
        name="mix",
    )(x, attn, y_ssm, zg, b_gate, w_a, w_m, w_o, ln_g, ln_b)


def _top_values(x, n):
    vals = []
    for _ in range(n):
        m = jnp.max(x, axis=0, keepdims=True)
        vals.append(m)
        x = jnp.where(x == m, -jnp.inf, x)
    return vals


def _gelu(x):
    return 0.5 * x * (1.0 + lax.erf(x * (2.0 ** -0.5)))


def _peer_kernel(h_ref, wqt_ref, k1_ref, k2_ref, u_ref, vt_ref, g_ref, b_ref, o_ref,
                 ht_scr, th_scr, s2_scr, e1_scr, e2_scr, wa_scr, acc_scr, *, alpha, i1_per_step):
    s = pl.program_id(1)
    n_blocks = pl.num_programs(1) - 1
    half = PEER_QDIM // 2
    nk = PEER_NKEYS

    @pl.when(s == 0)
    def _():
        ht = h_ref[...].T.astype(BF16)
        ht_scr[...] = ht
        k1 = k1_ref[...].astype(BF16)
        k2 = k2_ref[...].astype(BF16)
        for hd in range(PEER_HEADS):
            qt = _dot(wqt_ref[hd * PEER_QDIM:(hd + 1) * PEER_QDIM, :], ht).astype(BF16)
            s1 = _dot(k1, qt[:half])
            s2 = _dot(k2, qt[half:])
            v1 = _top_values(s1, PEER_TOPK)
            v2 = _top_values(s2, PEER_TOPK)
            cand = jnp.concatenate(
                [v1[i] + v2[jj] for i in range(PEER_TOPK) for jj in range(PEER_TOPK // (i + 1))], axis=0)
            top = _top_values(cand, PEER_TOPK)
            tau = top[PEER_TOPK - 1]
            zsum = sum(jnp.exp(t - top[0]) for t in top)
            theta = jnp.full(s1.shape, jnp.inf, F32)
            for jj in range(PEER_TOPK):
                theta = jnp.minimum(theta, jnp.where(s1 + v2[jj] >= tau, v2[jj], jnp.inf))
            th_scr[hd] = theta
            s2_scr[hd] = s2
            e1_scr[hd] = jnp.exp(s1 - v1[0])
            e2_scr[hd] = jnp.exp(s2 - v2[0]) / zsum
        acc_scr[...] = jnp.zeros(acc_scr.shape, F32)
        wa_scr[...] = jnp.zeros(wa_scr.shape, BF16)

    acc_scr[...] += _dot(vt_ref[...], wa_scr[(s + 1) % 2])

    blk = jnp.minimum(s, n_blocks - 1)
    act = _gelu(_dot(u_ref[...], ht_scr[...]))
    pieces = []
    for ii in range(i1_per_step):
        i1 = blk * i1_per_step + ii
        w = jnp.zeros((nk, act.shape[1]), F32)
        for hd in range(PEER_HEADS):
            th = th_scr[hd, pl.ds(i1, 1), :]
            cw = e1_scr[hd, pl.ds(i1, 1), :]
            w = w + jnp.where(s2_scr[hd] >= th, e2_scr[hd] * cw, 0.0)
        pieces.append((w * act[ii * nk:(ii + 1) * nk]).astype(BF16))
    wa_scr[s % 2] = jnp.concatenate(pieces, axis=0)

    @pl.when(s == pl.num_programs(1) - 1)
    def _():
        hblk = h_ref[...]
        o_ref[...] = _layer_norm(alpha * hblk + acc_scr[...].T, g_ref[...], b_ref[...])


def _peer(h, wq_t, keys1, keys2, u, v_t, ln_g, ln_b, *, alpha, tt, i1_per_step):
    m, d = h.shape
    n_exp = u.shape[0]
    eb = i1_per_step * PEER_NKEYS
    n_blocks = n_exp // eb
    par = lambda a: pl.BlockSpec(a.shape, lambda i, s: (0, 0))
    scr = lambda: pltpu.VMEM((PEER_HEADS, PEER_NKEYS, tt), F32)
    return pl.pallas_call(
        functools.partial(_peer_kernel, alpha=alpha, i1_per_step=i1_per_step),
        grid=(m // tt, n_blocks + 1),
        in_specs=[pl.BlockSpec((tt, d), lambda i, s: (i, 0)), par(wq_t), par(keys1), par(keys2),
                  pl.BlockSpec((eb, d), lambda i, s: (jnp.minimum(s, n_blocks - 1), 0)),
                  pl.BlockSpec((d, eb), lambda i, s: (0, jnp.maximum(s - 1, 0))),
                  par(ln_g), par(ln_b)],
        out_specs=pl.BlockSpec((tt, d), lambda i, s: (i, 0)),
        out_shape=jax.ShapeDtypeStruct((m, d), F32),
        scratch_shapes=[pltpu.VMEM((d, tt), BF16), scr(), scr(), scr(), scr(),
                        pltpu.VMEM((2, eb, tt), BF16), pltpu.VMEM((d, tt), F32)],
        compiler_params=_cparams("parallel", "arbitrary"),
        name="peer",
    )(h, wq_t, keys1, keys2, u, v_t, ln_g, ln_b)


def _pad_cols(w, width):
    return jnp.pad(w, ((0, 0), (0, width - w.shape[1])))


def _prep_weights(w_in, d_model, d_inner, conv_dim, n_ssm_heads):
    att_w = ATT_HEADS * ATT_HEAD_DIM
    kv_w = ATT_KV_HEADS * ATT_HEAD_DIM
    splits = (att_w, kv_w, kv_w, IDX_HEADS * IDX_DIM, IDX_DIM, IDX_HEADS,
              d_inner, conv_dim, n_ssm_heads, N_BRANCHES * d_model)
    offs = np.cumsum((0,) + splits)
    piece = lambda i: w_in[:, offs[i]:offs[i + 1]]
    w_a = jnp.concatenate([piece(0), piece(1), piece(2), piece(3),
                           _pad_cols(piece(4), LANES), _pad_cols(piece(5), LANES), _pad_cols(piece(8), LANES)],
                          axis=1).astype(BF16)
    w_xbc = piece(7).astype(BF16)
    w_zg = jnp.concatenate([piece(6), piece(9)], axis=1).astype(BF16)
    return w_a, w_xbc, w_zg


def _stream(x, pos, past, wts, *, tm, kt, tt):
    (w_a, w_xbc, w_zg, b_gate, conv_w, conv_b, dt_bias, a_log, d_skip_row, norm_w, w_att, w_ssm, w_out,
     ln1_g, ln1_b, wq_t, keys1, keys2, u, v_t, ln2_g, ln2_b, alpha) = wts
    b, l, d = x.shape
    m = b * l
    xf = x.reshape(m, d)
    conv_dim = w_xbc.shape[1]
    d_inner = norm_w.shape[1]

    tables = _rope_tables(jnp.asarray(np.tile(pos, max(1, tm // l))))
    q, k, v, qi, ki, wi, dt = _proj_a(xf, w_a, tables, tm)
    xbc = _matmul(xf, w_xbc, tm, 1024)
    zg = _matmul(xf, w_zg, tm, 1024)

    r3 = lambda a: a.reshape(b, l, a.shape[1])
    if past is None:
        k_all, v_all, ki_all = r3(k), r3(v), r3(ki)
        n_valid = l
        k_sel = min(TOPK_MAX, l // 4)
        conv_prev = jnp.zeros((b, SUBLANES, conv_dim), F32)
        ssm_prev = jnp.zeros((b, d_inner, SSM_D_STATE), F32)
    else:
        ck, cv, cki, conv_state, ssm_state = past
        n_past = ck.shape[1]
        n_valid = n_past + l
        n_pad = -n_valid % kt
        cat = lambda c, new: jnp.pad(jnp.concatenate([c.reshape(b, n_past, -1), r3(new)], axis=1),
                                     ((0, 0), (0, n_pad), (0, 0)))
        k_all, v_all, ki_all = cat(ck, k), cat(cv, v), cat(cki, ki)
        k_sel = min(TOPK_MAX, n_valid // 4)
        conv_prev = jnp.pad(conv_state, ((0, 0), (SUBLANES - (CONV_WIDTH - 1), 0), (0, 0)))
        ssm_prev = ssm_state.reshape(b, d_inner, SSM_D_STATE)
    q_pad = -l % LANES
    padq = lambda a: jnp.pad(r3(a), ((0, 0), (0, q_pad), (0, 0)))
    attn = _dsa(padq(q), padq(qi), padq(wi), k_all, v_all, ki_all,
                kt=kt, n_valid=n_valid, q_pos0=int(pos[0]), k_sel=k_sel)[:, :l]

    l_pad = -l % SSD_CHUNK
    padl = lambda a: jnp.pad(r3(a), ((0, 0), (0, l_pad), (0, 0)))
    y_ssm, conv_new, ssm_new = _ssd(padl(xbc), padl(zg), padl(dt), conv_prev, ssm_prev,
                                    conv_w, conv_b, dt_bias, a_log, d_skip_row, norm_w,
                                    l_valid=SSD_CHUNK if l_pad == 0 else l)
    y_ssm = y_ssm[:, :l].reshape(m, d_inner)
    conv_new = conv_new[:, SUBLANES - (CONV_WIDTH - 1):]
    n_ssm_heads = d_inner // SSM_HEAD_DIM
    ssm_new = ssm_new.reshape(b, n_ssm_heads, SSM_HEAD_DIM, SSM_D_STATE)

    h1 = _mix(xf, attn.reshape(m, -1), y_ssm, zg, b_gate, w_att, w_ssm, w_out, ln1_g, ln1_b,
              alpha=alpha, tm=tm)
    y = _peer(h1, wq_t, keys1, keys2, u, v_t, ln2_g, ln2_b, alpha=alpha, tt=tt, i1_per_step=4)
    state = (r3(k).reshape(b, l, ATT_KV_HEADS, ATT_HEAD_DIM), r3(v).reshape(b, l, ATT_KV_HEADS, ATT_HEAD_DIM),
             r3(ki), conv_new, ssm_new)
    return y.reshape(b, l, d), state


def kernel(x_prompt, x_sample, cache_k, cache_v, cache_kidx, state_conv, state_ssm, w_in, b_gate, conv_w, conv_b, dt_bias, a_log, d_skip, ssm_norm_w, w_attn_br, w_ssm_br, w_out, ln1_g, ln1_b, peer_wq, peer_keys1, peer_keys2, peer_u, peer_v, ln2_g, ln2_b):
    depth = w_in.shape[0]
    d_model = x_prompt.shape[2]
    d_inner = ssm_norm_w.shape[1]
    conv_dim = conv_w.shape[2]
    n_ssm_heads = a_log.shape[1]
    alpha = (2.0 * depth) ** 0.25
    pos_p = np.arange(x_prompt.shape[1])
    pos_s = cache_k.shape[2] + np.arange(x_sample.shape[1])

    hp, hs = x_prompt, x_sample
    new_p, new_s = [], []
    for l in range(depth):
        w_a, w_xbc, w_zg = _prep_weights(w_in[l], d_model, d_inner, conv_dim, n_ssm_heads)
        row = lambda a: a.reshape(1, -1)
        wts = (w_a, w_xbc, w_zg, row(b_gate[l]),
               jnp.pad(conv_w[l], ((0, SUBLANES - CONV_WIDTH), (0, 0))), row(conv_b[l]),
               _pad_cols(row(dt_bias[l]), LANES), _pad_cols(row(a_log[l]), LANES),
               row(jnp.repeat(d_skip[l], SSM_HEAD_DIM)), row(ssm_norm_w[l]),
               w_attn_br[l].astype(BF16), w_ssm_br[l].astype(BF16), w_out[l].astype(BF16),
               row(ln1_g[l]), row(ln1_b[l]),
               peer_wq[l].T.astype(BF16), peer_keys1[l], peer_keys2[l],
               peer_u[l].astype(BF16), peer_v[l].T.astype(BF16), row(ln2_g[l]), row(ln2_b[l]), alpha)
        hp, sp = _stream(hp, pos_p, None, wts, tm=512, kt=512, tt=512)
        past = (cache_k[l], cache_v[l], cache_kidx[l], state_conv[l], state_ssm[l])
        hs, ss = _stream(hs, pos_s, past, wts, tm=256, kt=384, tt=256)
        new_p.append(sp)
        new_s.append(ss)

    stack = lambda lst, i: jnp.stack([e[i] for e in lst], axis=0)
    return (hp, hs) + tuple(stack(new_p, i) for i in range(5)) + tuple(stack(new_s, i) for i in range(5))
```

```python
import functools

import jax
import jax.numpy as jnp
import numpy as np
from jax import lax
from jax.experimental import pallas as pl
from jax.experimental.pallas import tpu as pltpu

F32 = jnp.float32
BF16 = jnp.bfloat16

LANES = 128
SUBLANES = 8
VMEM_LIMIT_BYTES = 48 * 1024 * 1024

CHUNK = 64
ATT_HEADS = 16
ATT_HEAD_DIM = 64
ATT_KV_HEADS = 4
ATT_GROUP = ATT_HEADS // ATT_KV_HEADS
IDX_HEADS = 4
IDX_DIM = 64
TOPK_MAX = 256
ROPE_THETA = 500000.0
ROPE_FRACTION = 4
SSM_HEAD_DIM = 64
SSM_GROUPS = 4
SSM_D_STATE = 128
CONV_WIDTH = 4
N_BRANCHES = 2
PEER_HEADS = 8
PEER_NKEYS = 128
PEER_QDIM = 256
PEER_TOPK = 16
LN_EPS = 1e-5
RMS_EPS = 1e-5
NEG_INF = -1e30
SSD_CHUNK = 128
INT32_MIN = -(2 ** 31)
CODE_NEG_INFINITY = INT32_MIN + 0x7FFFFF


def _cparams(*sem):
    return pltpu.CompilerParams(dimension_semantics=sem, vmem_limit_bytes=VMEM_LIMIT_BYTES)


def _dot(a, b):
    return jnp.dot(a, b, preferred_element_type=F32)


def _dot_nt(a, b):
    return lax.dot_general(a, b, (((1,), (1,)), ((), ())), preferred_element_type=F32)


def _rope_tables(pos):
    rd = ATT_HEAD_DIM // ROPE_FRACTION
    half = rd // 2
    inv = ROPE_THETA ** (-(jnp.arange(half, dtype=F32) * 2.0) / rd)
    ang = pos.astype(F32)[:, None] * inv[None, :]
    cos, sin = jnp.cos(ang), jnp.sin(ang)
    n = pos.shape[0]
    ones = jnp.ones((n, ATT_HEAD_DIM - rd), F32)
    zeros = jnp.zeros((n, ATT_HEAD_DIM - rd), F32)
    zh = jnp.zeros((n, half), F32)
    cos_h = jnp.concatenate([cos, cos, ones], axis=1)
    sa_h = jnp.concatenate([-sin, zh, zeros], axis=1)
    sb_h = jnp.concatenate([zh, sin, zeros], axis=1)
    rep = LANES // ATT_HEAD_DIM
    return jnp.tile(cos_h, (1, rep)), jnp.tile(sa_h, (1, rep)), jnp.tile(sb_h, (1, rep))


def _proj_a_kernel(x_ref, w_ref, cos_ref, sa_ref, sb_ref,
                   q_ref, k_ref, v_ref, qi_ref, ki_ref, wi_ref, dt_ref):
    x = x_ref[...].astype(BF16)
    cos, sa, sb = cos_ref[...], sa_ref[...], sb_ref[...]
    half = ATT_HEAD_DIM // ROPE_FRACTION // 2

    def rope(t):
        up = pltpu.roll(t, LANES - half, 1)
        down = pltpu.roll(t, half, 1)
        return t * cos + up * sa + down * sb

    def tile(c):
        return _dot(x, w_ref[:, c * LANES:(c + 1) * LANES])

    nq = q_ref.shape[1] // LANES
    nk = k_ref.shape[1] // LANES
    c = 0
    for j in range(nq):
        q_ref[:, j * LANES:(j + 1) * LANES] = rope(tile(c + j))
    c += nq
    for j in range(nk):
        k_ref[:, j * LANES:(j + 1) * LANES] = rope(tile(c + j))
    c += nk
    for j in range(nk):
        v_ref[:, j * LANES:(j + 1) * LANES] = tile(c + j)
    c += nk
    nqi = qi_ref.shape[1] // LANES
    for j in range(nqi):
        qi_ref[:, j * LANES:(j + 1) * LANES] = rope(tile(c + j))
    c += nqi
    ki_ref[...] = rope(tile(c))[:, :IDX_DIM]
    wi_ref[...] = tile(c + 1)
    dt_ref[...] = tile(c + 2)


def _proj_a(x, w_a, tables, tm):
    m, d = x.shape
    att_w = ATT_HEADS * ATT_HEAD_DIM
    kv_w = ATT_KV_HEADS * ATT_HEAD_DIM
    idx_w = IDX_HEADS * IDX_DIM
    tab_blocks = tables[0].shape[0] // tm
    row = lambda i: (i, 0)
    tab = lambda i: (i % tab_blocks, 0)
    full = lambda i: (0, 0)
    out_w = (att_w, kv_w, kv_w, idx_w, IDX_DIM, LANES, LANES)
    return pl.pallas_call(
        _proj_a_kernel,
        grid=(m // tm,),
        in_specs=[pl.BlockSpec((tm, d), row), pl.BlockSpec(w_a.shape, full)]
                 + [pl.BlockSpec((tm, LANES), tab)] * 3,
        out_specs=[pl.BlockSpec((tm, w), row) for w in out_w],
        out_shape=[jax.ShapeDtypeStruct((m, w), F32) for w in out_w],
        compiler_params=_cparams("parallel"),
        name="proj_a",
    )(x, w_a, *tables)


def _matmul_kernel(x_ref, w_ref, o_ref):
    o_ref[...] = _dot(x_ref[...].astype(BF16), w_ref[...])


def _matmul(x, w, tm, tn):
    m, k = x.shape
    n = w.shape[1]
    return pl.pallas_call(
        _matmul_kernel,
        grid=(m // tm, n // tn),
        in_specs=[pl.BlockSpec((tm, k), lambda i, j: (i, 0)), pl.BlockSpec((k, tn), lambda i, j: (0, j))],
        out_specs=pl.BlockSpec((tm, tn), lambda i, j: (i, j)),
        out_shape=jax.ShapeDtypeStruct((m, n), F32),
        compiler_params=_cparams("parallel", "arbitrary"),
        name="matmul",
    )(x, w)


def _tree(op, xs):
    xs = list(xs)
    while len(xs) > 1:
        xs = [op(xs[i], xs[i + 1]) for i in range(0, len(xs) - 1, 2)] + ([xs[-1]] if len(xs) % 2 else [])
    return xs[0]


def _lane_tiles(x):
    return [x[:, c * LANES:(c + 1) * LANES] for c in range(x.shape[1] // LANES)]


def _dsa_kernel(q_ref, qi_ref, wi_ref, k_ref, v_ref, ki_ref, earlier_ref, o_ref,
                score_scr, sel_scr, lg_scr, q_scr, m_scr, l_scr, acc_scr,
                *, kt, nkt, n_valid, q_pos0, k_sel):
    qb = LANES
    j = pl.program_id(1)
    q_first = q_pos0 + j * qb
    nt = jnp.minimum(nkt, (q_first + qb + kt - 1) // kt)

    key_off = lax.broadcasted_iota(jnp.int32, (kt, qb), 0)
    q_chunk = (q_first + lax.broadcasted_iota(jnp.int32, (1, qb), 1)) // CHUNK
    key_limit = jnp.minimum((q_chunk + 1) * CHUNK, n_valid)

    def admissible(t):
        return key_off < key_limit - t * kt

    qi_t = qi_ref[...].T.astype(BF16)
    wi_t = wi_ref[...].T

    def score_tile(t, carry):
        ki_t = ki_ref[pl.ds(pl.multiple_of(t * kt, LANES), kt), :].astype(BF16)
        s = jnp.zeros((kt, qb), F32)
        for h in range(IDX_HEADS):
            lg = _dot(ki_t, qi_t[h * IDX_DIM:(h + 1) * IDX_DIM, :])
            s = s + jnp.maximum(lg, 0.0) * wi_t[h:h + 1, :]
        s = jnp.where(s == 0.0, 0.0, s)
        score_scr[t] = jnp.where(admissible(t), s, NEG_INF)
        return carry

    lax.fori_loop(0, nt, score_tile, 0)

    def count(pred):
        def body(t, acc):
            hit = jnp.where(pred(score_scr[t]), 1.0, 0.0)
            return acc + _tree(jnp.add, [hit[r:r + SUBLANES] for r in range(0, kt, SUBLANES)])
        acc = lax.fori_loop(0, nt, body, jnp.zeros((SUBLANES, qb), F32))
        return jnp.sum(acc, axis=0, keepdims=True)

    kf = jnp.float32(k_sel)

    def decode(code):
        bits = jnp.where(code < 0, code ^ jnp.int32(0x7FFFFFFF), code)
        return lax.bitcast_convert_type(bits, F32)

    def at_least_k(code):
        thr_f = decode(code)
        return jnp.logical_or(count(lambda sc: sc >= thr_f) >= kf, code < CODE_NEG_INFINITY)

    zero = jnp.zeros((1, qb), jnp.int32)
    code = jnp.where(count(lambda sc: sc >= 0.0) >= kf, zero, jnp.int32(INT32_MIN))

    def bit_pass(i, code):
        cand = code | jnp.left_shift(jnp.int32(1), 30 - i)
        return jnp.where(at_least_k(cand), cand, code)

    thr = decode(lax.fori_loop(0, 31, bit_pass, code))

    need = kf - count(lambda sc: sc > thr)
    earlier = earlier_ref[...]

    def select_tile(t, run):
        sc = score_scr[t]
        tie = sc == thr
        eq = jnp.where(tie, 1.0, 0.0)
        rank = _dot(earlier, eq.astype(BF16)) + run
        take = jnp.logical_or(sc > thr, jnp.logical_and(tie, rank < need))
        sel = jnp.where(jnp.logical_and(take, admissible(t)), 1.0, 0.0)
        sel_scr[t] = sel.T
        return run + jnp.sum(eq, axis=0, keepdims=True)

    lax.fori_loop(0, nt, select_tile, jnp.zeros((1, qb), F32))

    scale = ATT_HEAD_DIM ** -0.5
    hd = ATT_HEAD_DIM
    rows = ATT_GROUP * qb
    pair = LANES // hd
    for g in range(ATT_KV_HEADS):
        qg = jnp.concatenate(
            [q_ref[:, (g * ATT_GROUP + r) * hd:(g * ATT_GROUP + r + 1) * hd] for r in range(ATT_GROUP)],
            axis=0) * scale
        zeros = jnp.zeros_like(qg)
        parts = [qg if c == g % pair else zeros for c in range(pair)]
        q_scr[g] = jnp.concatenate(parts, axis=1).astype(BF16)
    m_scr[...] = jnp.full(m_scr.shape, NEG_INF, F32)
    l_scr[...] = jnp.zeros(l_scr.shape, F32)
    acc_scr[...] = jnp.zeros(acc_scr.shape, F32)

    def kv_tile(ref, t, g):
        lane0 = (g // pair) * LANES
        return ref[pl.ds(pl.multiple_of(t * kt, LANES), kt), lane0:lane0 + LANES].astype(BF16)

    def logits_tile(t, carry):
        sel = sel_scr[t][None] > 0.0
        for g in range(ATT_KV_HEADS):
            lg = _dot_nt(q_scr[g], kv_tile(k_ref, t, g)).reshape(ATT_GROUP, qb, kt)
            lg = jnp.where(sel, lg, NEG_INF).reshape(rows, kt)
            lg_scr[t, g] = lg
            m_scr[g] = jnp.maximum(m_scr[g], _tree(jnp.maximum, _lane_tiles(lg)))
        return carry

    lax.fori_loop(0, nt, logits_tile, 0)
    for g in range(ATT_KV_HEADS):
        m_scr[g] = jnp.broadcast_to(jnp.max(m_scr[g], axis=1, keepdims=True), (rows, LANES))

    def pv_tile(t, carry):
        for g in range(ATT_KV_HEADS):
            m = m_scr[g]
            ps = [jnp.exp(lg_c - m) for lg_c in _lane_tiles(lg_scr[t, g])]
            l_scr[g] += _tree(jnp.add, ps)
            acc_scr[g] += _dot(jnp.concatenate(ps, axis=1).astype(BF16), kv_tile(v_ref, t, g))
        return carry

    lax.fori_loop(0, nt, pv_tile, 0)
    for g in range(ATT_KV_HEADS):
        denom = jnp.sum(l_scr[g], axis=1, keepdims=True)
        out = acc_scr[g][:, (g % pair) * hd:(g % pair + 1) * hd] / denom
        for r in range(ATT_GROUP):
            h = g * ATT_GROUP + r
            o_ref[:, h * hd:(h + 1) * hd] = out[r * qb:(r + 1) * qb]


def _dsa(q, qi, wi, k, v, ki, *, kt, n_valid, q_pos0, k_sel):
    b, lq, _ = q.shape
    nk = k.shape[1]
    nkt = nk // kt
    qb = LANES
    qspec = lambda w: pl.BlockSpec((None, qb, w), lambda bi, j: (bi, j, 0))
    kspec = lambda w: pl.BlockSpec((None, nk, w), lambda bi, j: (bi, 0, 0))
    rows = ATT_GROUP * qb
    assert kt >= k_sel, "a query block must see at least k_sel key slots"
    earlier = jnp.tril(jnp.ones((kt, kt), BF16), -1)
    kern = functools.partial(_dsa_kernel, kt=kt, nkt=nkt, n_valid=n_valid, q_pos0=q_pos0, k_sel=k_sel)
    return pl.pallas_call(
        kern,
        grid=(b, lq // qb),
        in_specs=[qspec(q.shape[2]), qspec(qi.shape[2]), qspec(wi.shape[2]),
                  kspec(k.shape[2]), kspec(v.shape[2]), kspec(ki.shape[2]),
                  pl.BlockSpec((kt, kt), lambda bi, j: (0, 0))],
        out_specs=qspec(q.shape[2]),
        out_shape=jax.ShapeDtypeStruct(q.shape, F32),
        scratch_shapes=[
            pltpu.VMEM((nkt, kt, qb), F32),
            pltpu.VMEM((nkt, qb, kt), F32),
            pltpu.VMEM((nkt, ATT_KV_HEADS, rows, kt), F32),
            pltpu.VMEM((ATT_KV_HEADS, rows, LANES), BF16),
            pltpu.VMEM((ATT_KV_HEADS, rows, LANES), F32),
            pltpu.VMEM((ATT_KV_HEADS, rows, LANES), F32),
            pltpu.VMEM((ATT_KV_HEADS, rows, LANES), F32),
        ],
        compiler_params=_cparams("parallel", "arbitrary"),
        name="dsa",
    )(q, qi, wi, k, v, ki, earlier)


def _silu(x):
    return x / (1.0 + jnp.exp(-x))


def _ssd_kernel(xbc_ref, z_ref, dt_ref, cprev_ref, sprev_ref, cw_ref, cb_ref, dtb_ref, alog_ref,
                dsk_ref, nw_ref, y_ref, cnew_ref, s_ref, buf_ref, yh_ref, *, l_valid):
    c = pl.program_id(1)
    lc = SSD_CHUNK
    hist = SUBLANES
    d_inner = y_ref.shape[1]
    n_heads = d_inner // SSM_HEAD_DIM
    gn = SSM_GROUPS * SSM_D_STATE
    heads_per_group = n_heads // SSM_GROUPS
    p = SSM_HEAD_DIM

    @pl.when(c == 0)
    def _():
        buf_ref[0:hist, :] = cprev_ref[...]
        s_ref[...] = sprev_ref[...]

    buf_ref[hist:hist + lc, :] = xbc_ref[...]
    conv = cb_ref[...] + sum(
        buf_ref[hist - (CONV_WIDTH - 1) + t:hist - (CONV_WIDTH - 1) + t + lc, :] * cw_ref[t:t + 1, :]
        for t in range(CONV_WIDTH))
    xbc = _silu(conv)
    cnew_ref[...] = buf_ref[l_valid:l_valid + hist, :]
    buf_ref[0:hist, :] = buf_ref[lc:lc + hist, :]

    xs = xbc[:, :d_inner]
    b_all = xbc[:, d_inner:d_inner + gn].astype(BF16)
    c_all = xbc[:, d_inner + gn:].astype(BF16)

    row = lax.broadcasted_iota(jnp.int32, (lc, LANES), 0)
    pre = dt_ref[...] + dtb_ref[...]
    dt = jnp.maximum(pre, 0.0) + jnp.log1p(jnp.exp(-jnp.abs(pre)))
    dt = jnp.where(row < l_valid, dt, 0.0)
    a = dt * (-jnp.exp(alog_ref[...]))
    ta = lax.broadcasted_iota(jnp.int32, (lc, lc), 0)
    tb = lax.broadcasted_iota(jnp.int32, (lc, lc), 1)
    tril = ta >= tb
    acum = jnp.dot(jnp.where(tril, 1.0, 0.0), a, preferred_element_type=F32,
                   precision=lax.Precision.HIGHEST)
    acum_t = acum.T
    dt_t = dt.T
    xs_t = xs.T

    for g in range(SSM_GROUPS):
        b_g = b_all[:, g * SSM_D_STATE:(g + 1) * SSM_D_STATE]
        c_g = c_all[:, g * SSM_D_STATE:(g + 1) * SSM_D_STATE]
        cb = _dot_nt(c_g, b_g)
        for hh in range(heads_per_group):
            h = g * heads_per_group + hh
            col = acum[:, h:h + 1]
            arow = acum_t[h:h + 1, :]
            dtrow = dt_t[h:h + 1, :]
            alast = acum_t[h:h + 1, lc - 1:lc]
            decay = jnp.where(tril, jnp.exp(jnp.where(tril, col - arow, 0.0)), 0.0)
            mh = (cb * decay * dtrow).astype(BF16)
            y_diag = _dot(mh, xs[:, h * p:(h + 1) * p].astype(BF16))
            s_h = s_ref[h * p:(h + 1) * p, :]
            y_off = _dot_nt(c_g, s_h.astype(BF16)) * jnp.exp(col)
            yh_ref[:, h * p:(h + 1) * p] = y_diag + y_off
            w_row = dtrow * jnp.exp(alast - arow)
            x_t = (xs_t[h * p:(h + 1) * p, :] * w_row).astype(BF16)
            s_ref[h * p:(h + 1) * p, :] = s_h * jnp.exp(alast) + _dot(x_t, b_g)

    y = yh_ref[...] + dsk_ref[...] * xs
    y = y * _silu(z_ref[...])
    gw = d_inner // SSM_GROUPS
    for g in range(SSM_GROUPS):
        yg = y[:, g * gw:(g + 1) * gw]
        ms = jnp.mean(yg * yg, axis=1, keepdims=True)
        y_ref[:, g * gw:(g + 1) * gw] = yg * lax.rsqrt(ms + RMS_EPS) * nw_ref[:, g * gw:(g + 1) * gw]


def _ssd(xbc, z_src, dt, conv_prev, ssm_prev, conv_w, conv_b, dt_bias, a_log, d_skip, norm_w, *, l_valid):
    b, l, conv_dim = xbc.shape
    d_inner = norm_w.shape[1]
    nc = l // SSD_CHUNK
    tok = lambda w: pl.BlockSpec((None, SSD_CHUNK, w), lambda bi, c: (bi, c, 0))
    per_b = lambda s: pl.BlockSpec((None,) + s, lambda bi, c: (bi, 0, 0))
    par = lambda a: pl.BlockSpec(a.shape, lambda bi, c: (0, 0))
    kern = functools.partial(_ssd_kernel, l_valid=l_valid)
    return pl.pallas_call(
        kern,
        grid=(b, nc),
        in_specs=[tok(conv_dim), tok(d_inner), tok(LANES),
                  per_b((SUBLANES, conv_dim)), per_b(ssm_prev.shape[1:]),
                  par(conv_w), par(conv_b), par(dt_bias), par(a_log), par(d_skip), par(norm_w)],
        out_specs=[tok(d_inner), per_b((SUBLANES, conv_dim)), per_b(ssm_prev.shape[1:])],
        out_shape=[jax.ShapeDtypeStruct((b, l, d_inner), F32),
                   jax.ShapeDtypeStruct((b, SUBLANES, conv_dim), F32),
                   jax.ShapeDtypeStruct(ssm_prev.shape, F32)],
        scratch_shapes=[pltpu.VMEM((SSD_CHUNK + 2 * SUBLANES, conv_dim), F32),
                        pltpu.VMEM((SSD_CHUNK, d_inner), F32)],
        compiler_params=_cparams("parallel", "arbitrary"),
        name="ssd",
    )(xbc, z_src, dt, conv_prev, ssm_prev, conv_w, conv_b, dt_bias, a_log, d_skip, norm_w)


def _layer_norm(x, g, b):
    mu = jnp.mean(x, axis=-1, keepdims=True)
    xc = x - mu
    var = jnp.mean(xc * xc, axis=-1, keepdims=True)
    return xc * lax.rsqrt(var + LN_EPS) * g + b


def _mix_kernel(x_ref, attn_ref, yssm_ref, gate_ref, bg_ref, wa_ref, wm_ref, wo_ref, g_ref, b_ref, o_ref,
                *, alpha):
    d = x_ref.shape[1]
    ya = _dot(attn_ref[...].astype(BF16), wa_ref[...])
    ym = _dot(yssm_ref[...].astype(BF16), wm_ref[...])
    gates = gate_ref[...] + bg_ref[...]
    gates = 1.0 / (1.0 + jnp.exp(-gates))
    mixed = gates[:, :d] * ya + gates[:, d:] * ym
    out = _dot(mixed.astype(BF16), wo_ref[...])
    o_ref[...] = _layer_norm(alpha * x_ref[...] + out, g_ref[...], b_ref[...])


def _mix(x, attn, y_ssm, zg, b_gate, w_a, w_m, w_o, ln_g, ln_b, *, alpha, tm):
    m, d = x.shape
    gate_block = zg.shape[1] // (N_BRANCHES * d) - 1
    row = lambda w: pl.BlockSpec((tm, w), lambda i: (i, 0))
    par = lambda a: pl.BlockSpec(a.shape, lambda i: (0, 0))
    return pl.pallas_call(
        functools.partial(_mix_kernel, alpha=alpha),
        grid=(m // tm,),
        in_specs=[row(d), row(attn.shape[1]), row(y_ssm.shape[1]),
                  pl.BlockSpec((tm, N_BRANCHES * d), lambda i: (i, gate_block)),
                  par(b_gate), par(w_a), par(w_m), par(w_o), par(ln_g), par(ln_b)],
        out_specs=row(d),
        out_shape=jax.ShapeDtypeStruct((m, d), F32),
        compiler_params=_cparams("parallel"),
        name="mix",
    )(x, attn, y_ssm, zg, b_gate, w_a, w_m, w_o, ln_g, ln_b)


def _top_values(x, n):
    vals = []
    for _ in range(n):
        m = jnp.max(x, axis=0, keepdims=True)
        vals.append(m)
        x = jnp.where(x == m, -jnp.inf, x)
    return vals


def _gelu(x):
    return 0.5 * x * (1.0 + lax.erf(x * (2.0 ** -0.5)))


def _peer_kernel(h_ref, wqt_ref, k1_ref, k2_ref, u_ref, vt_ref, g_ref, b_ref, o_ref,
                 ht_scr, th_scr, s2_scr, e1_scr, e2_scr, act_scr, wa_scr, acc_scr, *, alpha, i1_per_step):
    s = pl.program_id(1)
    n_blocks = pl.num_programs(1) - 1
    half = PEER_QDIM // 2
    nk = PEER_NKEYS

    @pl.when(s == 0)
    def _():
        ht_scr[...] = h_ref[...].T.astype(BF16)

        def head(hd, carry):
            k1 = k1_ref[...].astype(BF16)
            k2 = k2_ref[...].astype(BF16)
            wq_h = wqt_ref[pl.ds(pl.multiple_of(hd * PEER_QDIM, PEER_QDIM), PEER_QDIM), :]
            qt = _dot(wq_h, ht_scr[...]).astype(BF16)
            s1 = _dot(k1, qt[:half])
            s2 = _dot(k2, qt[half:])
            v1 = _top_values(s1, PEER_TOPK)
            v2 = _top_values(s2, PEER_TOPK)
            cand = jnp.concatenate(
                [v1[i] + v2[jj] for i in range(PEER_TOPK) for jj in range(PEER_TOPK // (i + 1))], axis=0)
            top = _top_values(cand, PEER_TOPK)
            tau = top[PEER_TOPK - 1]
            zsum = sum(jnp.exp(t - top[0]) for t in top)
            theta = jnp.full(s1.shape, jnp.inf, F32)
            for jj in range(PEER_TOPK):
                theta = jnp.where(s1 + v2[jj] >= tau, v2[jj], theta)
            th_scr[hd] = theta
            s2_scr[hd] = s2
            e1_scr[hd] = jnp.exp(s1 - v1[0])
            e2_scr[hd] = jnp.exp(s2 - v2[0]) / zsum
            return carry

        lax.fori_loop(0, PEER_HEADS, head, 0)
        acc_scr[...] = jnp.zeros(acc_scr.shape, F32)
        act_scr[...] = jnp.zeros(act_scr.shape, F32)

    prev = jnp.maximum(s - 1, 0)
    tt = acc_scr.shape[1]
    n_cols = tt // LANES
    dk = ht_scr.shape[0] // n_cols
    rb = 4 * SUBLANES

    def u_chunk(c):
        return _dot(u_ref[:, c * dk:(c + 1) * dk], ht_scr[c * dk:(c + 1) * dk, :])

    def i1_rows(scr):
        return [[scr[hd, pl.ds(prev * i1_per_step + ii, 1), :] for ii in range(i1_per_step)]
                for hd in range(PEER_HEADS)]

    th_rows = i1_rows(th_scr)
    cw_rows = i1_rows(e1_scr)

    def gate_cols(c):
        cols = slice(c * LANES, (c + 1) * LANES)
        for r0 in range(0, nk, rb):
            w = [jnp.zeros((rb, LANES), F32) for _ in range(i1_per_step)]
            for hd in range(PEER_HEADS):
                s2t = s2_scr[hd, r0:r0 + rb, cols]
                e2t = e2_scr[hd, r0:r0 + rb, cols]
                for ii in range(i1_per_step):
                    th = th_rows[hd][ii][:, cols]
                    cw = cw_rows[hd][ii][:, cols]
                    w[ii] = w[ii] + jnp.where(s2t >= th, e2t * cw, 0.0)
            for ii in range(i1_per_step):
                rows = slice(ii * nk + r0, ii * nk + r0 + rb)
                wa_scr[rows, cols] = (w[ii] * act_scr[(s + 1) % 2, rows, cols]).astype(BF16)

    def second(c0, c1):
        cols = slice(c0 * LANES, c1 * LANES)
        acc_scr[:, cols] += _dot(vt_ref[...], wa_scr[:, cols])

    pre = None
    group = 2 if n_cols % 2 == 0 else 1
    for c in range(n_cols):
        uc = u_chunk(c)
        pre = uc if pre is None else pre + uc
        gate_cols(c)
        if (c + 1) % group == 0:
            second(c + 1 - group, c + 1)
    act_scr[s % 2] = _gelu(pre)

    @pl.when(s == pl.num_programs(1) - 1)
    def _():
        hblk = h_ref[...]
        o_ref[...] = _layer_norm(alpha * hblk + acc_scr[...].T, g_ref[...], b_ref[...])


def _peer(h, wq_t, keys1, keys2, u, v_t, ln_g, ln_b, *, alpha, tt, i1_per_step):
    m, d = h.shape
    n_exp = u.shape[0]
    eb = i1_per_step * PEER_NKEYS
    n_blocks = n_exp // eb
    par = lambda a: pl.BlockSpec(a.shape, lambda i, s: (0, 0))
    scr = lambda: pltpu.VMEM((PEER_HEADS, PEER_NKEYS, tt), F32)
    return pl.pallas_call(
        functools.partial(_peer_kernel, alpha=alpha, i1_per_step=i1_per_step),
        grid=(m // tt, n_blocks + 1),
        in_specs=[pl.BlockSpec((tt, d), lambda i, s: (i, 0)), par(wq_t), par(keys1), par(keys2),
                  pl.BlockSpec((eb, d), lambda i, s: (jnp.minimum(s, n_blocks - 1), 0)),
                  pl.BlockSpec((d, eb), lambda i, s: (0, jnp.maximum(s - 1, 0))),
                  par(ln_g), par(ln_b)],
        out_specs=pl.BlockSpec((tt, d), lambda i, s: (i, 0)),
        out_shape=jax.ShapeDtypeStruct((m, d), F32),
        scratch_shapes=[pltpu.VMEM((d, tt), BF16), scr(), scr(), scr(), scr(),
                        pltpu.VMEM((2, eb, tt), F32), pltpu.VMEM((eb, tt), BF16), pltpu.VMEM((d, tt), F32)],
        compiler_params=_cparams("parallel", "arbitrary"),
        name="peer",
    )(h, wq_t, keys1, keys2, u, v_t, ln_g, ln_b)


def _pad_cols(w, width):
    return jnp.pad(w, ((0, 0), (0, width - w.shape[1])))


def _prep_weights(w_in, d_model, d_inner, conv_dim, n_ssm_heads):
    att_w = ATT_HEADS * ATT_HEAD_DIM
    kv_w = ATT_KV_HEADS * ATT_HEAD_DIM
    splits = (att_w, kv_w, kv_w, IDX_HEADS * IDX_DIM, IDX_DIM, IDX_HEADS,
              d_inner, conv_dim, n_ssm_heads, N_BRANCHES * d_model)
    offs = np.cumsum((0,) + splits)
    piece = lambda i: w_in[:, offs[i]:offs[i + 1]]
    w_a = jnp.concatenate([piece(0), piece(1), piece(2), piece(3),
                           _pad_cols(piece(4), LANES), _pad_cols(piece(5), LANES), _pad_cols(piece(8), LANES)],
                          axis=1).astype(BF16)
    w_xbc = piece(7).astype(BF16)
    w_zg = jnp.concatenate([piece(6), piece(9)], axis=1).astype(BF16)
    return w_a, w_xbc, w_zg


def _tiles(m, n_keys):
    kt = min((512, 384, 256), key=lambda t: (-(-n_keys // t) * t, -t))
    return min(m, 512), min(m, 1024), min(m, 512), kt


def _stream(x, pos, past, wts):
    (w_a, w_xbc, w_zg, b_gate, conv_w, conv_b, dt_bias, a_log, d_skip_row, norm_w, w_att, w_ssm, w_out,
     ln1_g, ln1_b, wq_t, keys1, keys2, u, v_t, ln2_g, ln2_b, alpha) = wts
    b, l, d = x.shape
    m = b * l
    xf = x.reshape(m, d)
    conv_dim = w_xbc.shape[1]
    d_inner = norm_w.shape[1]
    tm, tm_mm, tt, kt = _tiles(m, l if past is None else past[0].shape[1] + l)

    tables = _rope_tables(jnp.asarray(np.tile(pos, max(1, tm // l))))
    q, k, v, qi, ki, wi, dt = _proj_a(xf, w_a, tables, tm)
    xbc = _matmul(xf, w_xbc, tm_mm, 1024)
    zg = _matmul(xf, w_zg, tm_mm, 1024)

    r3 = lambda a: a.reshape(b, l, a.shape[1])
    if past is None:
        k_all, v_all, ki_all = r3(k), r3(v), r3(ki)
        n_valid = l
        k_sel = min(TOPK_MAX, l // 4)
        conv_prev = jnp.zeros((b, SUBLANES, conv_dim), F32)
        ssm_prev = jnp.zeros((b, d_inner, SSM_D_STATE), F32)
    else:
        ck, cv, cki, conv_state, ssm_state = past
        n_past = ck.shape[1]
        n_valid = n_past + l
        n_pad = -n_valid % kt
        cat = lambda c, new: jnp.pad(jnp.concatenate([c.reshape(b, n_past, -1), r3(new)], axis=1),
                                     ((0, 0), (0, n_pad), (0, 0)))
        k_all, v_all, ki_all = cat(ck, k), cat(cv, v), cat(cki, ki)
        k_sel = min(TOPK_MAX, n_valid // 4)
        conv_prev = jnp.pad(conv_state, ((0, 0), (SUBLANES - (CONV_WIDTH - 1), 0), (0, 0)))
        ssm_prev = ssm_state.reshape(b, d_inner, SSM_D_STATE)
    q_pad = -l % LANES
    padq = lambda a: jnp.pad(r3(a), ((0, 0), (0, q_pad), (0, 0)))
    attn = _dsa(padq(q), padq(qi), padq(wi), k_all, v_all, ki_all,
                kt=kt, n_valid=n_valid, q_pos0=int(pos[0]), k_sel=k_sel)[:, :l]

    l_pad = -l % SSD_CHUNK
    padl = lambda a: jnp.pad(r3(a), ((0, 0), (0, l_pad), (0, 0)))
    y_ssm, conv_new, ssm_new = _ssd(padl(xbc), padl(zg), padl(dt), conv_prev, ssm_prev,
                                    conv_w, conv_b, dt_bias, a_log, d_skip_row, norm_w,
                                    l_valid=SSD_CHUNK if l_pad == 0 else l)
    y_ssm = y_ssm[:, :l].reshape(m, d_inner)
    conv_new = conv_new[:, SUBLANES - (CONV_WIDTH - 1):]
    n_ssm_heads = d_inner // SSM_HEAD_DIM
    ssm_new = ssm_new.reshape(b, n_ssm_heads, SSM_HEAD_DIM, SSM_D_STATE)

    h1 = _mix(xf, attn.reshape(m, -1), y_ssm, zg, b_gate, w_att, w_ssm, w_out, ln1_g, ln1_b,
              alpha=alpha, tm=tm)
    y = _peer(h1, wq_t, keys1, keys2, u, v_t, ln2_g, ln2_b, alpha=alpha, tt=tt, i1_per_step=4)
    state = (r3(k).reshape(b, l, ATT_KV_HEADS, ATT_HEAD_DIM), r3(v).reshape(b, l, ATT_KV_HEADS, ATT_HEAD_DIM),
             r3(ki), conv_new, ssm_new)
    return y.reshape(b, l, d), state


def kernel(x_prompt, x_sample, cache_k, cache_v, cache_kidx, state_conv, state_ssm, w_in, b_gate, conv_w, conv_b, dt_bias, a_log, d_skip, ssm_norm_w, w_attn_br, w_ssm_br, w_out, ln1_g, ln1_b, peer_wq, peer_keys1, peer_keys2, peer_u, peer_v, ln2_g, ln2_b):
    depth = w_in.shape[0]
    d_model = x_prompt.shape[2]
    d_inner = ssm_norm_w.shape[1]
    conv_dim = conv_w.shape[2]
    n_ssm_heads = a_log.shape[1]
    alpha = (2.0 * depth) ** 0.25
    pos_p = np.arange(x_prompt.shape[1])
    pos_s = cache_k.shape[2] + np.arange(x_sample.shape[1])

    hp, hs = x_prompt, x_sample
    new_p, new_s = [], []
    for l in range(depth):
        w_a, w_xbc, w_zg = _prep_weights(w_in[l], d_model, d_inner, conv_dim, n_ssm_heads)
        row = lambda a: a.reshape(1, -1)
        wts = (w_a, w_xbc, w_zg, row(b_gate[l]),
               jnp.pad(conv_w[l], ((0, SUBLANES - CONV_WIDTH), (0, 0))), row(conv_b[l]),
               _pad_cols(row(dt_bias[l]), LANES), _pad_cols(row(a_log[l]), LANES),
               row(jnp.repeat(d_skip[l], SSM_HEAD_DIM)), row(ssm_norm_w[l]),
               w_attn_br[l].astype(BF16), w_ssm_br[l].astype(BF16), w_out[l].astype(BF16),
               row(ln1_g[l]), row(ln1_b[l]),
               peer_wq[l].T.astype(BF16), peer_keys1[l], peer_keys2[l],
               peer_u[l].astype(BF16), peer_v[l].astype(BF16).T, row(ln2_g[l]), row(ln2_b[l]), alpha)
        hp, sp = _stream(hp, pos_p, None, wts)
        past = (cache_k[l], cache_v[l], cache_kidx[l], state_conv[l], state_ssm[l])
        hs, ss = _stream(hs, pos_s, past, wts)
        new_p.append(sp)
        new_s.append(ss)

    stack = lambda lst, i: jnp.stack([e[i] for e in lst], axis=0)
    return (hp, hs) + tuple(stack(new_p, i) for i in range(5)) + tuple(stack(new_s, i) for i in range(5))
```

```python
import functools

import jax
import jax.numpy as jnp
import numpy as np
from jax import lax
from jax.experimental import pallas as pl
from jax.experimental.pallas import tpu as pltpu

F32 = jnp.float32
BF16 = jnp.bfloat16

LANES = 128
SUBLANES = 8
VMEM_LIMIT_BYTES = 48 * 1024 * 1024

CHUNK = 64
ATT_HEADS = 16
ATT_HEAD_DIM = 64
ATT_KV_HEADS = 4
ATT_GROUP = ATT_HEADS // ATT_KV_HEADS
IDX_HEADS = 4
IDX_DIM = 64
TOPK_MAX = 256
ROPE_THETA = 500000.0
ROPE_FRACTION = 4
SSM_HEAD_DIM = 64
SSM_GROUPS = 4
SSM_D_STATE = 128
CONV_WIDTH = 4
N_BRANCHES = 2
PEER_HEADS = 8
PEER_NKEYS = 128
PEER_QDIM = 256
PEER_TOPK = 16
LN_EPS = 1e-5
RMS_EPS = 1e-5
NEG_INF = -1e30
SSD_CHUNK = 128
INT32_MIN = -(2 ** 31)
CODE_NEG_INFINITY = INT32_MIN + 0x7FFFFF


def _cparams(*sem):
    return pltpu.CompilerParams(dimension_semantics=sem, vmem_limit_bytes=VMEM_LIMIT_BYTES)


def _dot(a, b):
    return jnp.dot(a, b, preferred_element_type=F32)


def _dot_nt(a, b):
    return lax.dot_general(a, b, (((1,), (1,)), ((), ())), preferred_element_type=F32)


def _rope_tables(pos):
    rd = ATT_HEAD_DIM // ROPE_FRACTION
    half = rd // 2
    inv = ROPE_THETA ** (-(jnp.arange(half, dtype=F32) * 2.0) / rd)
    ang = pos.astype(F32)[:, None] * inv[None, :]
    cos, sin = jnp.cos(ang), jnp.sin(ang)
    n = pos.shape[0]
    ones = jnp.ones((n, ATT_HEAD_DIM - rd), F32)
    zeros = jnp.zeros((n, ATT_HEAD_DIM - rd), F32)
    zh = jnp.zeros((n, half), F32)
    cos_h = jnp.concatenate([cos, cos, ones], axis=1)
    sa_h = jnp.concatenate([-sin, zh, zeros], axis=1)
    sb_h = jnp.concatenate([zh, sin, zeros], axis=1)
    rep = LANES // ATT_HEAD_DIM
    return jnp.tile(cos_h, (1, rep)), jnp.tile(sa_h, (1, rep)), jnp.tile(sb_h, (1, rep))


def _proj_a_kernel(x_ref, w_ref, cos_ref, sa_ref, sb_ref,
                   q_ref, k_ref, v_ref, qi_ref, ki_ref, wi_ref, dt_ref):
    x = x_ref[...].astype(BF16)
    cos, sa, sb = cos_ref[...], sa_ref[...], sb_ref[...]
    half = ATT_HEAD_DIM // ROPE_FRACTION // 2

    def rope(t):
        up = pltpu.roll(t, LANES - half, 1)
        down = pltpu.roll(t, half, 1)
        return t * cos + up * sa + down * sb

    def tile(c):
        return _dot(x, w_ref[:, c * LANES:(c + 1) * LANES])

    nq = q_ref.shape[1] // LANES
    nk = k_ref.shape[1] // LANES
    c = 0
    for j in range(nq):
        q_ref[:, j * LANES:(j + 1) * LANES] = rope(tile(c + j))
    c += nq
    for j in range(nk):
        k_ref[:, j * LANES:(j + 1) * LANES] = rope(tile(c + j))
    c += nk
    for j in range(nk):
        v_ref[:, j * LANES:(j + 1) * LANES] = tile(c + j)
    c += nk
    nqi = qi_ref.shape[1] // LANES
    for j in range(nqi):
        qi_ref[:, j * LANES:(j + 1) * LANES] = rope(tile(c + j))
    c += nqi
    ki_ref[...] = rope(tile(c))[:, :IDX_DIM]
    wi_ref[...] = tile(c + 1)
    dt_ref[...] = tile(c + 2)


def _proj_a(x, w_a, tables, tm):
    m, d = x.shape
    att_w = ATT_HEADS * ATT_HEAD_DIM
    kv_w = ATT_KV_HEADS * ATT_HEAD_DIM
    idx_w = IDX_HEADS * IDX_DIM
    tab_blocks = tables[0].shape[0] // tm
    row = lambda i: (i, 0)
    tab = lambda i: (i % tab_blocks, 0)
    full = lambda i: (0, 0)
    out_w = (att_w, kv_w, kv_w, idx_w, IDX_DIM, LANES, LANES)
    return pl.pallas_call(
        _proj_a_kernel,
        grid=(m // tm,),
        in_specs=[pl.BlockSpec((tm, d), row), pl.BlockSpec(w_a.shape, full)]
                 + [pl.BlockSpec((tm, LANES), tab)] * 3,
        out_specs=[pl.BlockSpec((tm, w), row) for w in out_w],
        out_shape=[jax.ShapeDtypeStruct((m, w), F32) for w in out_w],
        compiler_params=_cparams("parallel"),
        name="proj_a",
    )(x, w_a, *tables)


def _matmul_kernel(x_ref, w_ref, o_ref):
    o_ref[...] = _dot(x_ref[...].astype(BF16), w_ref[...])


def _matmul(x, w, tm, tn):
    m, k = x.shape
    n = w.shape[1]
    return pl.pallas_call(
        _matmul_kernel,
        grid=(m // tm, n // tn),
        in_specs=[pl.BlockSpec((tm, k), lambda i, j: (i, 0)), pl.BlockSpec((k, tn), lambda i, j: (0, j))],
        out_specs=pl.BlockSpec((tm, tn), lambda i, j: (i, j)),
        out_shape=jax.ShapeDtypeStruct((m, n), F32),
        compiler_params=_cparams("parallel", "arbitrary"),
        name="matmul",
    )(x, w)


def _tree(op, xs):
    xs = list(xs)
    while len(xs) > 1:
        xs = [op(xs[i], xs[i + 1]) for i in range(0, len(xs) - 1, 2)] + ([xs[-1]] if len(xs) % 2 else [])
    return xs[0]


def _lane_tiles(x):
    return [x[:, c * LANES:(c + 1) * LANES] for c in range(x.shape[1] // LANES)]


def _dsa_kernel(q_ref, qi_ref, wi_ref, k_ref, v_ref, ki_ref, earlier_ref, o_ref,
                score_scr, sel_scr, lg_scr, q_scr, m_scr, l_scr, acc_scr,
                *, kt, nkt, n_valid, q_pos0, k_sel):
    qb = LANES
    j = pl.program_id(1)
    q_first = q_pos0 + j * qb
    nt = jnp.minimum(nkt, (q_first + qb + kt - 1) // kt)

    key_off = lax.broadcasted_iota(jnp.int32, (kt, qb), 0)
    q_chunk = (q_first + lax.broadcasted_iota(jnp.int32, (1, qb), 1)) // CHUNK
    key_limit = jnp.minimum((q_chunk + 1) * CHUNK, n_valid)

    def admissible(t):
        return key_off < key_limit - t * kt

    qi_t = qi_ref[...].T.astype(BF16)
    wi_t = wi_ref[...].T

    def score_tile(t, carry):
        ki_t = ki_ref[pl.ds(pl.multiple_of(t * kt, LANES), kt), :].astype(BF16)
        s = jnp.zeros((kt, qb), F32)
        for h in range(IDX_HEADS):
            lg = _dot(ki_t, qi_t[h * IDX_DIM:(h + 1) * IDX_DIM, :])
            s = s + jnp.maximum(lg, 0.0) * wi_t[h:h + 1, :]
        s = jnp.where(s == 0.0, 0.0, s)
        score_scr[t] = jnp.where(admissible(t), s, NEG_INF)
        return carry

    lax.fori_loop(0, nt, score_tile, 0)

    def count(pred):
        def body(t, acc):
            hit = jnp.where(pred(score_scr[t]), 1.0, 0.0)
            return acc + _tree(jnp.add, [hit[r:r + SUBLANES] for r in range(0, kt, SUBLANES)])
        acc = lax.fori_loop(0, nt, body, jnp.zeros((SUBLANES, qb), F32))
        return jnp.sum(acc, axis=0, keepdims=True)

    kf = jnp.float32(k_sel)

    def decode(code):
        bits = jnp.where(code < 0, code ^ jnp.int32(0x7FFFFFFF), code)
        return lax.bitcast_convert_type(bits, F32)

    def at_least_k(code):
        thr_f = decode(code)
        return jnp.logical_or(count(lambda sc: sc >= thr_f) >= kf, code < CODE_NEG_INFINITY)

    zero = jnp.zeros((1, qb), jnp.int32)
    code = jnp.where(count(lambda sc: sc >= 0.0) >= kf, zero, jnp.int32(INT32_MIN))

    def bit_pass(i, code):
        cand = code | jnp.left_shift(jnp.int32(1), 30 - i)
        return jnp.where(at_least_k(cand), cand, code)

    thr = decode(lax.fori_loop(0, 31, bit_pass, code))

    need = kf - count(lambda sc: sc > thr)
    earlier = earlier_ref[...]

    def select_tile(t, run):
        sc = score_scr[t]
        tie = sc == thr
        eq = jnp.where(tie, 1.0, 0.0)
        rank = _dot(earlier, eq.astype(BF16)) + run
        take = jnp.logical_or(sc > thr, jnp.logical_and(tie, rank < need))
        sel = jnp.where(jnp.logical_and(take, admissible(t)), 1.0, 0.0)
        sel_scr[t] = sel.T
        return run + jnp.sum(eq, axis=0, keepdims=True)

    lax.fori_loop(0, nt, select_tile, jnp.zeros((1, qb), F32))

    scale = ATT_HEAD_DIM ** -0.5
    hd = ATT_HEAD_DIM
    rows = ATT_GROUP * qb
    pair = LANES // hd
    for g in range(ATT_KV_HEADS):
        qg = jnp.concatenate(
            [q_ref[:, (g * ATT_GROUP + r) * hd:(g * ATT_GROUP + r + 1) * hd] for r in range(ATT_GROUP)],
            axis=0) * scale
        zeros = jnp.zeros_like(qg)
        parts = [qg if c == g % pair else zeros for c in range(pair)]
        q_scr[g] = jnp.concatenate(parts, axis=1).astype(BF16)
    m_scr[...] = jnp.full(m_scr.shape, NEG_INF, F32)
    l_scr[...] = jnp.zeros(l_scr.shape, F32)
    acc_scr[...] = jnp.zeros(acc_scr.shape, F32)

    def kv_tile(ref, t, g):
        lane0 = (g // pair) * LANES
        return ref[pl.ds(pl.multiple_of(t * kt, LANES), kt), lane0:lane0 + LANES].astype(BF16)

    def logits_tile(t, carry):
        sel = sel_scr[t][None] > 0.0
        for g in range(ATT_KV_HEADS):
            lg = _dot_nt(q_scr[g], kv_tile(k_ref, t, g)).reshape(ATT_GROUP, qb, kt)
            lg = jnp.where(sel, lg, NEG_INF).reshape(rows, kt)
            lg_scr[t, g] = lg
            m_scr[g] = jnp.maximum(m_scr[g], _tree(jnp.maximum, _lane_tiles(lg)))
        return carry

    lax.fori_loop(0, nt, logits_tile, 0)
    for g in range(ATT_KV_HEADS):
        m_scr[g] = jnp.broadcast_to(jnp.max(m_scr[g], axis=1, keepdims=True), (rows, LANES))

    def pv_tile(t, carry):
        for g in range(ATT_KV_HEADS):
            m = m_scr[g]
            ps = [jnp.exp(lg_c - m) for lg_c in _lane_tiles(lg_scr[t, g])]
            l_scr[g] += _tree(jnp.add, ps)
            acc_scr[g] += _dot(jnp.concatenate(ps, axis=1).astype(BF16), kv_tile(v_ref, t, g))
        return carry

    lax.fori_loop(0, nt, pv_tile, 0)
    for g in range(ATT_KV_HEADS):
        denom = jnp.sum(l_scr[g], axis=1, keepdims=True)
        out = acc_scr[g][:, (g % pair) * hd:(g % pair + 1) * hd] / denom
        for r in range(ATT_GROUP):
            h = g * ATT_GROUP + r
            o_ref[:, h * hd:(h + 1) * hd] = out[r * qb:(r + 1) * qb]


def _dsa(q, qi, wi, k, v, ki, *, kt, n_valid, q_pos0, k_sel):
    b, lq, _ = q.shape
    nk = k.shape[1]
    nkt = nk // kt
    qb = LANES
    qspec = lambda w: pl.BlockSpec((None, qb, w), lambda bi, j: (bi, j, 0))
    kspec = lambda w: pl.BlockSpec((None, nk, w), lambda bi, j: (bi, 0, 0))
    rows = ATT_GROUP * qb
    assert kt >= k_sel, "a query block must see at least k_sel key slots"
    earlier = jnp.tril(jnp.ones((kt, kt), BF16), -1)
    kern = functools.partial(_dsa_kernel, kt=kt, nkt=nkt, n_valid=n_valid, q_pos0=q_pos0, k_sel=k_sel)
    return pl.pallas_call(
        kern,
        grid=(b, lq // qb),
        in_specs=[qspec(q.shape[2]), qspec(qi.shape[2]), qspec(wi.shape[2]),
                  kspec(k.shape[2]), kspec(v.shape[2]), kspec(ki.shape[2]),
                  pl.BlockSpec((kt, kt), lambda bi, j: (0, 0))],
        out_specs=qspec(q.shape[2]),
        out_shape=jax.ShapeDtypeStruct(q.shape, F32),
        scratch_shapes=[
            pltpu.VMEM((nkt, kt, qb), F32),
            pltpu.VMEM((nkt, qb, kt), F32),
            pltpu.VMEM((nkt, ATT_KV_HEADS, rows, kt), F32),
            pltpu.VMEM((ATT_KV_HEADS, rows, LANES), BF16),
            pltpu.VMEM((ATT_KV_HEADS, rows, LANES), F32),
            pltpu.VMEM((ATT_KV_HEADS, rows, LANES), F32),
            pltpu.VMEM((ATT_KV_HEADS, rows, LANES), F32),
        ],
        compiler_params=_cparams("parallel", "arbitrary"),
        name="dsa",
    )(q, qi, wi, k, v, ki, earlier)


def _silu(x):
    return x / (1.0 + jnp.exp(-x))


def _ssd_kernel(xbc_ref, z_ref, dt_ref, cprev_ref, sprev_ref, cw_ref, cb_ref, dtb_ref, alog_ref,
                dsk_ref, nw_ref, y_ref, cnew_ref, s_ref, buf_ref, yh_ref, *, l_valid):
    c = pl.program_id(1)
    lc = SSD_CHUNK
    hist = SUBLANES
    d_inner = y_ref.shape[1]
    n_heads = d_inner // SSM_HEAD_DIM
    gn = SSM_GROUPS * SSM_D_STATE
    heads_per_group = n_heads // SSM_GROUPS
    p = SSM_HEAD_DIM

    @pl.when(c == 0)
    def _():
        buf_ref[0:hist, :] = cprev_ref[...]
        s_ref[...] = sprev_ref[...]

    buf_ref[hist:hist + lc, :] = xbc_ref[...]
    conv = cb_ref[...] + sum(
        buf_ref[hist - (CONV_WIDTH - 1) + t:hist - (CONV_WIDTH - 1) + t + lc, :] * cw_ref[t:t + 1, :]
        for t in range(CONV_WIDTH))
    xbc = _silu(conv)
    cnew_ref[...] = buf_ref[l_valid:l_valid + hist, :]
    buf_ref[0:hist, :] = buf_ref[lc:lc + hist, :]

    xs = xbc[:, :d_inner]
    b_all = xbc[:, d_inner:d_inner + gn].astype(BF16)
    c_all = xbc[:, d_inner + gn:].astype(BF16)

    row = lax.broadcasted_iota(jnp.int32, (lc, LANES), 0)
    pre = dt_ref[...] + dtb_ref[...]
    dt = jnp.maximum(pre, 0.0) + jnp.log1p(jnp.exp(-jnp.abs(pre)))
    dt = jnp.where(row < l_valid, dt, 0.0)
    a = dt * (-jnp.exp(alog_ref[...]))
    ta = lax.broadcasted_iota(jnp.int32, (lc, lc), 0)
    tb = lax.broadcasted_iota(jnp.int32, (lc, lc), 1)
    tril = ta >= tb
    acum = jnp.dot(jnp.where(tril, 1.0, 0.0), a, preferred_element_type=F32,
                   precision=lax.Precision.HIGHEST)
    acum_t = acum.T
    dt_t = dt.T
    xs_t = xs.T

    for g in range(SSM_GROUPS):
        b_g = b_all[:, g * SSM_D_STATE:(g + 1) * SSM_D_STATE]
        c_g = c_all[:, g * SSM_D_STATE:(g + 1) * SSM_D_STATE]
        cb = _dot_nt(c_g, b_g)
        for hh in range(heads_per_group):
            h = g * heads_per_group + hh
            col = acum[:, h:h + 1]
            arow = acum_t[h:h + 1, :]
            dtrow = dt_t[h:h + 1, :]
            alast = acum_t[h:h + 1, lc - 1:lc]
            decay = jnp.where(tril, jnp.exp(jnp.where(tril, col - arow, 0.0)), 0.0)
            mh = (cb * decay * dtrow).astype(BF16)
            y_diag = _dot(mh, xs[:, h * p:(h + 1) * p].astype(BF16))
            s_h = s_ref[h * p:(h + 1) * p, :]
            y_off = _dot_nt(c_g, s_h.astype(BF16)) * jnp.exp(col)
            yh_ref[:, h * p:(h + 1) * p] = y_diag + y_off
            w_row = dtrow * jnp.exp(alast - arow)
            x_t = (xs_t[h * p:(h + 1) * p, :] * w_row).astype(BF16)
            s_ref[h * p:(h + 1) * p, :] = s_h * jnp.exp(alast) + _dot(x_t, b_g)

    y = yh_ref[...] + dsk_ref[...] * xs
    y = y * _silu(z_ref[...])
    gw = d_inner // SSM_GROUPS
    for g in range(SSM_GROUPS):
        yg = y[:, g * gw:(g + 1) * gw]
        ms = jnp.mean(yg * yg, axis=1, keepdims=True)
        y_ref[:, g * gw:(g + 1) * gw] = yg * lax.rsqrt(ms + RMS_EPS) * nw_ref[:, g * gw:(g + 1) * gw]


def _ssd(xbc, z_src, dt, conv_prev, ssm_prev, conv_w, conv_b, dt_bias, a_log, d_skip, norm_w, *, l_valid):
    b, l, conv_dim = xbc.shape
    d_inner = norm_w.shape[1]
    nc = l // SSD_CHUNK
    tok = lambda w: pl.BlockSpec((None, SSD_CHUNK, w), lambda bi, c: (bi, c, 0))
    per_b = lambda s: pl.BlockSpec((None,) + s, lambda bi, c: (bi, 0, 0))
    par = lambda a: pl.BlockSpec(a.shape, lambda bi, c: (0, 0))
    kern = functools.partial(_ssd_kernel, l_valid=l_valid)
    return pl.pallas_call(
        kern,
        grid=(b, nc),
        in_specs=[tok(conv_dim), tok(d_inner), tok(LANES),
                  per_b((SUBLANES, conv_dim)), per_b(ssm_prev.shape[1:]),
                  par(conv_w), par(conv_b), par(dt_bias), par(a_log), par(d_skip), par(norm_w)],
        out_specs=[tok(d_inner), per_b((SUBLANES, conv_dim)), per_b(ssm_prev.shape[1:])],
        out_shape=[jax.ShapeDtypeStruct((b, l, d_inner), F32),
                   jax.ShapeDtypeStruct((b, SUBLANES, conv_dim), F32),
                   jax.ShapeDtypeStruct(ssm_prev.shape, F32)],
        scratch_shapes=[pltpu.VMEM((SSD_CHUNK + 2 * SUBLANES, conv_dim), F32),
                        pltpu.VMEM((SSD_CHUNK, d_inner), F32)],
        compiler_params=_cparams("parallel", "arbitrary"),
        name="ssd",
    )(xbc, z_src, dt, conv_prev, ssm_prev, conv_w, conv_b, dt_bias, a_log, d_skip, norm_w)


def _layer_norm(x, g, b):
    mu = jnp.mean(x, axis=-1, keepdims=True)
    xc = x - mu
    var = jnp.mean(xc * xc, axis=-1, keepdims=True)
    return xc * lax.rsqrt(var + LN_EPS) * g + b


def _mix_kernel(x_ref, attn_ref, yssm_ref, gate_ref, bg_ref, wa_ref, wm_ref, wo_ref, g_ref, b_ref, o_ref,
                *, alpha):
    d = x_ref.shape[1]
    ya = _dot(attn_ref[...].astype(BF16), wa_ref[...])
    ym = _dot(yssm_ref[...].astype(BF16), wm_ref[...])
    gates = gate_ref[...] + bg_ref[...]
    gates = 1.0 / (1.0 + jnp.exp(-gates))
    mixed = gates[:, :d] * ya + gates[:, d:] * ym
    out = _dot(mixed.astype(BF16), wo_ref[...])
    o_ref[...] = _layer_norm(alpha * x_ref[...] + out, g_ref[...], b_ref[...])


def _mix(x, attn, y_ssm, zg, b_gate, w_a, w_m, w_o, ln_g, ln_b, *, alpha, tm):
    m, d = x.shape
    gate_block = zg.shape[1] // (N_BRANCHES * d) - 1
    row = lambda w: pl.BlockSpec((tm, w), lambda i: (i, 0))
    par = lambda a: pl.BlockSpec(a.shape, lambda i: (0, 0))
    return pl.pallas_call(
        functools.partial(_mix_kernel, alpha=alpha),
        grid=(m // tm,),
        in_specs=[row(d), row(attn.shape[1]), row(y_ssm.shape[1]),
                  pl.BlockSpec((tm, N_BRANCHES * d), lambda i: (i, gate_block)),
                  par(b_gate), par(w_a), par(w_m), par(w_o), par(ln_g), par(ln_b)],
        out_specs=row(d),
        out_shape=jax.ShapeDtypeStruct((m, d), F32),
        compiler_params=_cparams("parallel"),
        name="mix",
    )(x, attn, y_ssm, zg, b_gate, w_a, w_m, w_o, ln_g, ln_b)


def _top_values(x, n, with_rank=False):
    vals = []
    rank = jnp.full(x.shape, float(n), F32)
    for i in range(n):
        m = jnp.max(x, axis=0, keepdims=True)
        vals.append(m)
        hit = x == m
        if with_rank:
            rank = jnp.where(hit, float(i), rank)
        x = jnp.where(hit, -jnp.inf, x)
    return (vals, rank) if with_rank else vals


def _gelu(x):
    return 0.5 * x * (1.0 + lax.erf(x * (2.0 ** -0.5)))


def _peer_kernel(h_ref, wqt_ref, k1_ref, k2_ref, u_ref, vt_ref, g_ref, b_ref, o_ref,
                 ht_scr, nsel_scr, rank_scr, e1_scr, e2_scr, act_scr, wa_scr, acc_scr, *, alpha, i1_per_step):
    s = pl.program_id(1)
    n_blocks = pl.num_programs(1) - 1
    half = PEER_QDIM // 2
    nk = PEER_NKEYS

    @pl.when(s == 0)
    def _():
        ht = h_ref[...].T.astype(BF16)
        ht_scr[...] = ht
        k1 = k1_ref[...].astype(BF16)
        k2 = k2_ref[...].astype(BF16)
        for hd in range(PEER_HEADS):
            qt = _dot(wqt_ref[hd * PEER_QDIM:(hd + 1) * PEER_QDIM, :], ht).astype(BF16)
            s1 = _dot(k1, qt[:half])
            s2 = _dot(k2, qt[half:])
            v1 = _top_values(s1, PEER_TOPK)
            v2, rank2 = _top_values(s2, PEER_TOPK, with_rank=True)
            cand = jnp.concatenate(
                [v1[i] + v2[jj] for i in range(PEER_TOPK) for jj in range(PEER_TOPK // (i + 1))], axis=0)
            top = _top_values(cand, PEER_TOPK)
            tau = top[PEER_TOPK - 1]
            zsum = sum(jnp.exp(t - top[0]) for t in top)
            n_sel = jnp.zeros(s1.shape, F32)
            for jj in range(PEER_TOPK):
                n_sel = jnp.where(s1 + v2[jj] >= tau, float(jj + 1), n_sel)
            nsel_scr[hd] = n_sel
            rank_scr[hd] = rank2.astype(BF16)
            e1_scr[hd] = jnp.exp(s1 - v1[0])
            e2_scr[hd] = (jnp.exp(s2 - v2[0]) / zsum).astype(BF16)
        acc_scr[...] = jnp.zeros(acc_scr.shape, F32)
        act_scr[...] = jnp.zeros(act_scr.shape, F32)

    prev = jnp.maximum(s - 1, 0)
    tt = acc_scr.shape[1]
    n_cols = tt // LANES
    dk = ht_scr.shape[0] // n_cols
    rb = 4 * SUBLANES

    def u_chunk(c):
        return _dot(u_ref[:, c * dk:(c + 1) * dk], ht_scr[c * dk:(c + 1) * dk, :])

    def i1_rows(scr):
        return [[scr[hd, pl.ds(prev * i1_per_step + ii, 1), :].astype(BF16) for ii in range(i1_per_step)]
                for hd in range(PEER_HEADS)]

    nsel_rows = i1_rows(nsel_scr)
    cw_rows = i1_rows(e1_scr)

    def gate_cols(c):
        cols = slice(c * LANES, (c + 1) * LANES)
        for r0 in range(0, nk, rb):
            w = [jnp.zeros((rb, LANES), BF16) for _ in range(i1_per_step)]
            for hd in range(PEER_HEADS):
                rk = rank_scr[hd, r0:r0 + rb, cols]
                e2t = e2_scr[hd, r0:r0 + rb, cols]
                for ii in range(i1_per_step):
                    ns = nsel_rows[hd][ii][:, cols]
                    cw = cw_rows[hd][ii][:, cols]
                    w[ii] = w[ii] + jnp.where(rk < ns, e2t * cw, jnp.zeros((), BF16))
            for ii in range(i1_per_step):
                rows = slice(ii * nk + r0, ii * nk + r0 + rb)
                wa_scr[rows, cols] = w[ii] * act_scr[(s + 1) % 2, rows, cols].astype(BF16)

    def second(c0, c1):
        cols = slice(c0 * LANES, c1 * LANES)
        acc_scr[:, cols] += _dot(vt_ref[...], wa_scr[:, cols])

    pre = None
    group = 2 if n_cols % 2 == 0 else 1
    for c in range(n_cols):
        uc = u_chunk(c)
        pre = uc if pre is None else pre + uc
        gate_cols(c)
        if (c + 1) % group == 0:
            second(c + 1 - group, c + 1)
    act_scr[s % 2] = _gelu(pre)

    @pl.when(s == pl.num_programs(1) - 1)
    def _():
        hblk = h_ref[...]
        o_ref[...] = _layer_norm(alpha * hblk + acc_scr[...].T, g_ref[...], b_ref[...])


def _peer(h, wq_t, keys1, keys2, u, v_t, ln_g, ln_b, *, alpha, tt, i1_per_step):
    m, d = h.shape
    n_exp = u.shape[0]
    eb = i1_per_step * PEER_NKEYS
    n_blocks = n_exp // eb
    par = lambda a: pl.BlockSpec(a.shape, lambda i, s: (0, 0))
    scr = lambda dt: pltpu.VMEM((PEER_HEADS, PEER_NKEYS, tt), dt)
    return pl.pallas_call(
        functools.partial(_peer_kernel, alpha=alpha, i1_per_step=i1_per_step),
        grid=(m // tt, n_blocks + 1),
        in_specs=[pl.BlockSpec((tt, d), lambda i, s: (i, 0)), par(wq_t), par(keys1), par(keys2),
                  pl.BlockSpec((eb, d), lambda i, s: (jnp.minimum(s, n_blocks - 1), 0)),
                  pl.BlockSpec((d, eb), lambda i, s: (0, jnp.maximum(s - 1, 0))),
                  par(ln_g), par(ln_b)],
        out_specs=pl.BlockSpec((tt, d), lambda i, s: (i, 0)),
        out_shape=jax.ShapeDtypeStruct((m, d), F32),
        scratch_shapes=[pltpu.VMEM((d, tt), BF16), scr(F32), scr(BF16), scr(F32), scr(BF16),
                        pltpu.VMEM((2, eb, tt), F32), pltpu.VMEM((eb, tt), BF16), pltpu.VMEM((d, tt), F32)],
        compiler_params=_cparams("parallel", "arbitrary"),
        name="peer",
    )(h, wq_t, keys1, keys2, u, v_t, ln_g, ln_b)


def _pad_cols(w, width):
    return jnp.pad(w, ((0, 0), (0, width - w.shape[1])))


def _prep_weights(w_in, d_model, d_inner, conv_dim, n_ssm_heads):
    att_w = ATT_HEADS * ATT_HEAD_DIM
    kv_w = ATT_KV_HEADS * ATT_HEAD_DIM
    splits = (att_w, kv_w, kv_w, IDX_HEADS * IDX_DIM, IDX_DIM, IDX_HEADS,
              d_inner, conv_dim, n_ssm_heads, N_BRANCHES * d_model)
    offs = np.cumsum((0,) + splits)
    piece = lambda i: w_in[:, offs[i]:offs[i + 1]]
    w_a = jnp.concatenate([piece(0), piece(1), piece(2), piece(3),
                           _pad_cols(piece(4), LANES), _pad_cols(piece(5), LANES), _pad_cols(piece(8), LANES)],
                          axis=1).astype(BF16)
    w_xbc = piece(7).astype(BF16)
    w_zg = jnp.concatenate([piece(6), piece(9)], axis=1).astype(BF16)
    return w_a, w_xbc, w_zg


def _tiles(m, n_keys):
    kt = min((512, 384, 256), key=lambda t: (-(-n_keys // t) * t, -t))
    return min(m, 512), min(m, 1024), min(m, 512), kt


def _stream(x, pos, past, wts):
    (w_a, w_xbc, w_zg, b_gate, conv_w, conv_b, dt_bias, a_log, d_skip_row, norm_w, w_att, w_ssm, w_out,
     ln1_g, ln1_b, wq_t, keys1, keys2, u, v_t, ln2_g, ln2_b, alpha) = wts
    b, l, d = x.shape
    m = b * l
    xf = x.reshape(m, d)
    conv_dim = w_xbc.shape[1]
    d_inner = norm_w.shape[1]
    tm, tm_mm, tt, kt = _tiles(m, l if past is None else past[0].shape[1] + l)

    tables = _rope_tables(jnp.asarray(np.tile(pos, max(1, tm // l))))
    q, k, v, qi, ki, wi, dt = _proj_a(xf, w_a, tables, tm)
    xbc = _matmul(xf, w_xbc, tm_mm, 1024)
    zg = _matmul(xf, w_zg, tm_mm, 1024)

    r3 = lambda a: a.reshape(b, l, a.shape[1])
    if past is None:
        k_all, v_all, ki_all = r3(k), r3(v), r3(ki)
        n_valid = l
        k_sel = min(TOPK_MAX, l // 4)
        conv_prev = jnp.zeros((b, SUBLANES, conv_dim), F32)
        ssm_prev = jnp.zeros((b, d_inner, SSM_D_STATE), F32)
    else:
        ck, cv, cki, conv_state, ssm_state = past
        n_past = ck.shape[1]
        n_valid = n_past + l
        n_pad = -n_valid % kt
        cat = lambda c, new: jnp.pad(jnp.concatenate([c.reshape(b, n_past, -1), r3(new)], axis=1),
                                     ((0, 0), (0, n_pad), (0, 0)))
        k_all, v_all, ki_all = cat(ck, k), cat(cv, v), cat(cki, ki)
        k_sel = min(TOPK_MAX, n_valid // 4)
        conv_prev = jnp.pad(conv_state, ((0, 0), (SUBLANES - (CONV_WIDTH - 1), 0), (0, 0)))
        ssm_prev = ssm_state.reshape(b, d_inner, SSM_D_STATE)
    q_pad = -l % LANES
    padq = lambda a: jnp.pad(r3(a), ((0, 0), (0, q_pad), (0, 0)))
    attn = _dsa(padq(q), padq(qi), padq(wi), k_all, v_all, ki_all,
                kt=kt, n_valid=n_valid, q_pos0=int(pos[0]), k_sel=k_sel)[:, :l]

    l_pad = -l % SSD_CHUNK
    padl = lambda a: jnp.pad(r3(a), ((0, 0), (0, l_pad), (0, 0)))
    y_ssm, conv_new, ssm_new = _ssd(padl(xbc), padl(zg), padl(dt), conv_prev, ssm_prev,
                                    conv_w, conv_b, dt_bias, a_log, d_skip_row, norm_w,
                                    l_valid=SSD_CHUNK if l_pad == 0 else l)
    y_ssm = y_ssm[:, :l].reshape(m, d_inner)
    conv_new = conv_new[:, SUBLANES - (CONV_WIDTH - 1):]
    n_ssm_heads = d_inner // SSM_HEAD_DIM
    ssm_new = ssm_new.reshape(b, n_ssm_heads, SSM_HEAD_DIM, SSM_D_STATE)

    h1 = _mix(xf, attn.reshape(m, -1), y_ssm, zg, b_gate, w_att, w_ssm, w_out, ln1_g, ln1_b,
              alpha=alpha, tm=tm)
    y = _peer(h1, wq_t, keys1, keys2, u, v_t, ln2_g, ln2_b, alpha=alpha, tt=tt, i1_per_step=4)
    state = (r3(k).reshape(b, l, ATT_KV_HEADS, ATT_HEAD_DIM), r3(v).reshape(b, l, ATT_KV_HEADS, ATT_HEAD_DIM),
             r3(ki), conv_new, ssm_new)
    return y.reshape(b, l, d), state


def kernel(x_prompt, x_sample, cache_k, cache_v, cache_kidx, state_conv, state_ssm, w_in, b_gate, conv_w, conv_b, dt_bias, a_log, d_skip, ssm_norm_w, w_attn_br, w_ssm_br, w_out, ln1_g, ln1_b, peer_wq, peer_keys1, peer_keys2, peer_u, peer_v, ln2_g, ln2_b):
    depth = w_in.shape[0]
    d_model = x_prompt.shape[2]
    d_inner = ssm_norm_w.shape[1]
    conv_dim = conv_w.shape[2]
    n_ssm_heads = a_log.shape[1]
    alpha = (2.0 * depth) ** 0.25
    pos_p = np.arange(x_prompt.shape[1])
    pos_s = cache_k.shape[2] + np.arange(x_sample.shape[1])

    hp, hs = x_prompt, x_sample
    new_p, new_s = [], []
    for l in range(depth):
        w_a, w_xbc, w_zg = _prep_weights(w_in[l], d_model, d_inner, conv_dim, n_ssm_heads)
        row = lambda a: a.reshape(1, -1)
        wts = (w_a, w_xbc, w_zg, row(b_gate[l]),
               jnp.pad(conv_w[l], ((0, SUBLANES - CONV_WIDTH), (0, 0))), row(conv_b[l]),
               _pad_cols(row(dt_bias[l]), LANES), _pad_cols(row(a_log[l]), LANES),
               row(jnp.repeat(d_skip[l], SSM_HEAD_DIM)), row(ssm_norm_w[l]),
               w_attn_br[l].astype(BF16), w_ssm_br[l].astype(BF16), w_out[l].astype(BF16),
               row(ln1_g[l]), row(ln1_b[l]),
               peer_wq[l].T.astype(BF16), peer_keys1[l], peer_keys2[l],
               peer_u[l].astype(BF16), peer_v[l].astype(BF16).T, row(ln2_g[l]), row(ln2_b[l]), alpha)
        hp, sp = _stream(hp, pos_p, None, wts)
        past = (cache_k[l], cache_v[l], cache_kidx[l], state_conv[l], state_ssm[l])
        hs, ss = _stream(hs, pos_s, past, wts)
        new_p.append(sp)
        new_s.append(ss)

    stack = lambda lst, i: jnp.stack([e[i] for e in lst], axis=0)
    return (hp, hs) + tuple(stack(new_p, i) for i in range(5)) + tuple(stack(new_s, i) for i in range(5))
```

```python
import functools

import jax
import jax.numpy as jnp
import numpy as np
from jax import lax
from jax.experimental import pallas as pl
from jax.experimental.pallas import tpu as pltpu

F32 = jnp.float32
BF16 = jnp.bfloat16

LANES = 128
SUBLANES = 8
VMEM_LIMIT_BYTES = 48 * 1024 * 1024

CHUNK = 64
ATT_HEADS = 16
ATT_HEAD_DIM = 64
ATT_KV_HEADS = 4
ATT_GROUP = ATT_HEADS // ATT_KV_HEADS
IDX_HEADS = 4
IDX_DIM = 64
TOPK_MAX = 256
ROPE_THETA = 500000.0
ROPE_FRACTION = 4
SSM_HEAD_DIM = 64
SSM_GROUPS = 4
SSM_D_STATE = 128
CONV_WIDTH = 4
N_BRANCHES = 2
PEER_HEADS = 8
PEER_NKEYS = 128
PEER_QDIM = 256
PEER_TOPK = 16
LN_EPS = 1e-5
RMS_EPS = 1e-5
NEG_INF = -1e30
SSD_CHUNK = 128
INT32_MIN = -(2 ** 31)
CODE_NEG_INFINITY = INT32_MIN + 0x7FFFFF


def _cparams(*sem):
    return pltpu.CompilerParams(dimension_semantics=sem, vmem_limit_bytes=VMEM_LIMIT_BYTES)


def _dot(a, b):
    return jnp.dot(a, b, preferred_element_type=F32)


def _dot_nt(a, b):
    return lax.dot_general(a, b, (((1,), (1,)), ((), ())), preferred_element_type=F32)


def _rope_tables(pos):
    rd = ATT_HEAD_DIM // ROPE_FRACTION
    half = rd // 2
    inv = ROPE_THETA ** (-(jnp.arange(half, dtype=F32) * 2.0) / rd)
    ang = pos.astype(F32)[:, None] * inv[None, :]
    cos, sin = jnp.cos(ang), jnp.sin(ang)
    n = pos.shape[0]
    ones = jnp.ones((n, ATT_HEAD_DIM - rd), F32)
    zeros = jnp.zeros((n, ATT_HEAD_DIM - rd), F32)
    zh = jnp.zeros((n, half), F32)
    cos_h = jnp.concatenate([cos, cos, ones], axis=1)
    sa_h = jnp.concatenate([-sin, zh, zeros], axis=1)
    sb_h = jnp.concatenate([zh, sin, zeros], axis=1)
    rep = LANES // ATT_HEAD_DIM
    return jnp.tile(cos_h, (1, rep)), jnp.tile(sa_h, (1, rep)), jnp.tile(sb_h, (1, rep))


def _proj_a_kernel(x_ref, w_ref, cos_ref, sa_ref, sb_ref,
                   q_ref, k_ref, v_ref, qi_ref, ki_ref, wi_ref, dt_ref):
    x = x_ref[...].astype(BF16)
    cos, sa, sb = cos_ref[...], sa_ref[...], sb_ref[...]
    half = ATT_HEAD_DIM // ROPE_FRACTION // 2

    def rope(t):
        up = pltpu.roll(t, LANES - half, 1)
        down = pltpu.roll(t, half, 1)
        return t * cos + up * sa + down * sb

    def tile(c):
        return _dot(x, w_ref[:, c * LANES:(c + 1) * LANES])

    nq = q_ref.shape[1] // LANES
    nk = k_ref.shape[1] // LANES
    c = 0
    for j in range(nq):
        q_ref[:, j * LANES:(j + 1) * LANES] = rope(tile(c + j))
    c += nq
    for j in range(nk):
        k_ref[:, j * LANES:(j + 1) * LANES] = rope(tile(c + j))
    c += nk
    for j in range(nk):
        v_ref[:, j * LANES:(j + 1) * LANES] = tile(c + j)
    c += nk
    nqi = qi_ref.shape[1] // LANES
    for j in range(nqi):
        qi_ref[:, j * LANES:(j + 1) * LANES] = rope(tile(c + j))
    c += nqi
    ki_ref[...] = rope(tile(c))[:, :IDX_DIM]
    wi_ref[...] = tile(c + 1)
    dt_ref[...] = tile(c + 2)


def _proj_a(x, w_a, tables, tm):
    m, d = x.shape
    att_w = ATT_HEADS * ATT_HEAD_DIM
    kv_w = ATT_KV_HEADS * ATT_HEAD_DIM
    idx_w = IDX_HEADS * IDX_DIM
    tab_blocks = tables[0].shape[0] // tm
    row = lambda i: (i, 0)
    tab = lambda i: (i % tab_blocks, 0)
    full = lambda i: (0, 0)
    out_w = (att_w, kv_w, kv_w, idx_w, IDX_DIM, LANES, LANES)
    return pl.pallas_call(
        _proj_a_kernel,
        grid=(m // tm,),
        in_specs=[pl.BlockSpec((tm, d), row), pl.BlockSpec(w_a.shape, full)]
                 + [pl.BlockSpec((tm, LANES), tab)] * 3,
        out_specs=[pl.BlockSpec((tm, w), row) for w in out_w],
        out_shape=[jax.ShapeDtypeStruct((m, w), F32) for w in out_w],
        compiler_params=_cparams("parallel"),
        name="proj_a",
    )(x, w_a, *tables)


def _matmul_kernel(x_ref, w_ref, o_ref):
    o_ref[...] = _dot(x_ref[...].astype(BF16), w_ref[...])


def _matmul(x, w, tm, tn):
    m, k = x.shape
    n = w.shape[1]
    return pl.pallas_call(
        _matmul_kernel,
        grid=(m // tm, n // tn),
        in_specs=[pl.BlockSpec((tm, k), lambda i, j: (i, 0)), pl.BlockSpec((k, tn), lambda i, j: (0, j))],
        out_specs=pl.BlockSpec((tm, tn), lambda i, j: (i, j)),
        out_shape=jax.ShapeDtypeStruct((m, n), F32),
        compiler_params=_cparams("parallel", "arbitrary"),
        name="matmul",
    )(x, w)


def _tree(op, xs):
    xs = list(xs)
    while len(xs) > 1:
        xs = [op(xs[i], xs[i + 1]) for i in range(0, len(xs) - 1, 2)] + ([xs[-1]] if len(xs) % 2 else [])
    return xs[0]


def _lane_tiles(x):
    return [x[:, c * LANES:(c + 1) * LANES] for c in range(x.shape[1] // LANES)]


def _dsa_kernel(q_ref, qi_ref, wi_ref, k_ref, v_ref, ki_ref, earlier_ref, o_ref,
                score_scr, sel_scr, lg_scr, q_scr, m_scr, l_scr, acc_scr,
                *, kt, nkt, n_valid, q_pos0, k_sel):
    qb = LANES
    j = pl.program_id(1)
    q_first = q_pos0 + j * qb
    nt = jnp.minimum(nkt, (q_first + qb + kt - 1) // kt)

    key_off = lax.broadcasted_iota(jnp.int32, (kt, qb), 0)
    q_chunk = (q_first + lax.broadcasted_iota(jnp.int32, (1, qb), 1)) // CHUNK
    key_limit = jnp.minimum((q_chunk + 1) * CHUNK, n_valid)

    def admissible(t):
        return key_off < key_limit - t * kt

    qi_t = qi_ref[...].T.astype(BF16)
    wi_t = wi_ref[...].T

    def score_tile(t, carry):
        ki_t = ki_ref[pl.ds(pl.multiple_of(t * kt, LANES), kt), :].astype(BF16)
        s = jnp.zeros((kt, qb), F32)
        for h in range(IDX_HEADS):
            lg = _dot(ki_t, qi_t[h * IDX_DIM:(h + 1) * IDX_DIM, :])
            s = s + jnp.maximum(lg, 0.0) * wi_t[h:h + 1, :]
        s = jnp.where(s == 0.0, 0.0, s)
        score_scr[t] = jnp.where(admissible(t), s, NEG_INF)
        return carry

    lax.fori_loop(0, nt, score_tile, 0)

    def count(pred):
        def body(t, acc):
            hit = jnp.where(pred(score_scr[t]), 1.0, 0.0)
            return acc + _tree(jnp.add, [hit[r:r + SUBLANES] for r in range(0, kt, SUBLANES)])
        acc = lax.fori_loop(0, nt, body, jnp.zeros((SUBLANES, qb), F32))
        return jnp.sum(acc, axis=0, keepdims=True)

    kf = jnp.float32(k_sel)

    def decode(code):
        bits = jnp.where(code < 0, code ^ jnp.int32(0x7FFFFFFF), code)
        return lax.bitcast_convert_type(bits, F32)

    def at_least_k(code):
        thr_f = decode(code)
        return jnp.logical_or(count(lambda sc: sc >= thr_f) >= kf, code < CODE_NEG_INFINITY)

    zero = jnp.zeros((1, qb), jnp.int32)
    code = jnp.where(count(lambda sc: sc >= 0.0) >= kf, zero, jnp.int32(INT32_MIN))

    def bit_pass(i, code):
        cand = code | jnp.left_shift(jnp.int32(1), 30 - i)
        return jnp.where(at_least_k(cand), cand, code)

    thr = decode(lax.fori_loop(0, 31, bit_pass, code))

    need = kf - count(lambda sc: sc > thr)
    earlier = earlier_ref[...]

    def select_tile(t, run):
        sc = score_scr[t]
        tie = sc == thr
        eq = jnp.where(tie, 1.0, 0.0)
        rank = _dot(earlier, eq.astype(BF16)) + run
        take = jnp.logical_or(sc > thr, jnp.logical_and(tie, rank < need))
        sel = jnp.where(jnp.logical_and(take, admissible(t)), 1.0, 0.0)
        sel_scr[t] = sel.T
        return run + jnp.sum(eq, axis=0, keepdims=True)

    lax.fori_loop(0, nt, select_tile, jnp.zeros((1, qb), F32))

    scale = ATT_HEAD_DIM ** -0.5
    hd = ATT_HEAD_DIM
    rows = ATT_GROUP * qb
    pair = LANES // hd
    for g in range(ATT_KV_HEADS):
        qg = jnp.concatenate(
            [q_ref[:, (g * ATT_GROUP + r) * hd:(g * ATT_GROUP + r + 1) * hd] for r in range(ATT_GROUP)],
            axis=0) * scale
        zeros = jnp.zeros_like(qg)
        parts = [qg if c == g % pair else zeros for c in range(pair)]
        q_scr[g] = jnp.concatenate(parts, axis=1).astype(BF16)
    m_scr[...] = jnp.full(m_scr.shape, NEG_INF, F32)
    l_scr[...] = jnp.zeros(l_scr.shape, F32)
    acc_scr[...] = jnp.zeros(acc_scr.shape, F32)

    def kv_tile(ref, t, g):
        lane0 = (g // pair) * LANES
        return ref[pl.ds(pl.multiple_of(t * kt, LANES), kt), lane0:lane0 + LANES].astype(BF16)

    def logits_tile(t, carry):
        sel = sel_scr[t][None] > 0.0
        for g in range(ATT_KV_HEADS):
            lg = _dot_nt(q_scr[g], kv_tile(k_ref, t, g)).reshape(ATT_GROUP, qb, kt)
            lg = jnp.where(sel, lg, NEG_INF).reshape(rows, kt)
            lg_scr[t, g] = lg
            m_scr[g] = jnp.maximum(m_scr[g], _tree(jnp.maximum, _lane_tiles(lg)))
        return carry

    lax.fori_loop(0, nt, logits_tile, 0)
    for g in range(ATT_KV_HEADS):
        m_scr[g] = jnp.broadcast_to(jnp.max(m_scr[g], axis=1, keepdims=True), (rows, LANES))

    def pv_tile(t, carry):
        for g in range(ATT_KV_HEADS):
            m = m_scr[g]
            ps = [jnp.exp(lg_c - m) for lg_c in _lane_tiles(lg_scr[t, g])]
            l_scr[g] += _tree(jnp.add, ps)
            acc_scr[g] += _dot(jnp.concatenate(ps, axis=1).astype(BF16), kv_tile(v_ref, t, g))
        return carry

    lax.fori_loop(0, nt, pv_tile, 0)
    for g in range(ATT_KV_HEADS):
        denom = jnp.sum(l_scr[g], axis=1, keepdims=True)
        out = acc_scr[g][:, (g % pair) * hd:(g % pair + 1) * hd] / denom
        for r in range(ATT_GROUP):
            h = g * ATT_GROUP + r
            o_ref[:, h * hd:(h + 1) * hd] = out[r * qb:(r + 1) * qb]


def _dsa(q, qi, wi, k, v, ki, *, kt, n_valid, q_pos0, k_sel):
    b, lq, _ = q.shape
    nk = k.shape[1]
    nkt = nk // kt
    qb = LANES
    qspec = lambda w: pl.BlockSpec((None, qb, w), lambda bi, j: (bi, j, 0))
    kspec = lambda w: pl.BlockSpec((None, nk, w), lambda bi, j: (bi, 0, 0))
    rows = ATT_GROUP * qb
    assert kt >= k_sel, "a query block must see at least k_sel key slots"
    earlier = jnp.tril(jnp.ones((kt, kt), BF16), -1)
    kern = functools.partial(_dsa_kernel, kt=kt, nkt=nkt, n_valid=n_valid, q_pos0=q_pos0, k_sel=k_sel)
    return pl.pallas_call(
        kern,
        grid=(b, lq // qb),
        in_specs=[qspec(q.shape[2]), qspec(qi.shape[2]), qspec(wi.shape[2]),
                  kspec(k.shape[2]), kspec(v.shape[2]), kspec(ki.shape[2]),
                  pl.BlockSpec((kt, kt), lambda bi, j: (0, 0))],
        out_specs=qspec(q.shape[2]),
        out_shape=jax.ShapeDtypeStruct(q.shape, F32),
        scratch_shapes=[
            pltpu.VMEM((nkt, kt, qb), F32),
            pltpu.VMEM((nkt, qb, kt), F32),
            pltpu.VMEM((nkt, ATT_KV_HEADS, rows, kt), F32),
            pltpu.VMEM((ATT_KV_HEADS, rows, LANES), BF16),
            pltpu.VMEM((ATT_KV_HEADS, rows, LANES), F32),
            pltpu.VMEM((ATT_KV_HEADS, rows, LANES), F32),
            pltpu.VMEM((ATT_KV_HEADS, rows, LANES), F32),
        ],
        compiler_params=_cparams("parallel", "arbitrary"),
        name="dsa",
    )(q, qi, wi, k, v, ki, earlier)


def _silu(x):
    return x / (1.0 + jnp.exp(-x))


def _ssd_kernel(xbc_ref, z_ref, dt_ref, cprev_ref, sprev_ref, cw_ref, cb_ref, dtb_ref, alog_ref,
                dsk_ref, nw_ref, y_ref, cnew_ref, s_ref, buf_ref, yh_ref, *, l_valid):
    c = pl.program_id(1)
    lc = SSD_CHUNK
    hist = SUBLANES
    d_inner = y_ref.shape[1]
    n_heads = d_inner // SSM_HEAD_DIM
    gn = SSM_GROUPS * SSM_D_STATE
    heads_per_group = n_heads // SSM_GROUPS
    p = SSM_HEAD_DIM

    @pl.when(c == 0)
    def _():
        buf_ref[0:hist, :] = cprev_ref[...]
        s_ref[...] = sprev_ref[...]

    buf_ref[hist:hist + lc, :] = xbc_ref[...]
    conv = cb_ref[...] + sum(
        buf_ref[hist - (CONV_WIDTH - 1) + t:hist - (CONV_WIDTH - 1) + t + lc, :] * cw_ref[t:t + 1, :]
        for t in range(CONV_WIDTH))
    xbc = _silu(conv)
    cnew_ref[...] = buf_ref[l_valid:l_valid + hist, :]
    buf_ref[0:hist, :] = buf_ref[lc:lc + hist, :]

    xs = xbc[:, :d_inner]
    b_all = xbc[:, d_inner:d_inner + gn].astype(BF16)
    c_all = xbc[:, d_inner + gn:].astype(BF16)

    row = lax.broadcasted_iota(jnp.int32, (lc, LANES), 0)
    pre = dt_ref[...] + dtb_ref[...]
    dt = jnp.maximum(pre, 0.0) + jnp.log1p(jnp.exp(-jnp.abs(pre)))
    dt = jnp.where(row < l_valid, dt, 0.0)
    a = dt * (-jnp.exp(alog_ref[...]))
    ta = lax.broadcasted_iota(jnp.int32, (lc, lc), 0)
    tb = lax.broadcasted_iota(jnp.int32, (lc, lc), 1)
    tril = ta >= tb
    acum = jnp.dot(jnp.where(tril, 1.0, 0.0), a, preferred_element_type=F32,
                   precision=lax.Precision.HIGHEST)
    acum_t = acum.T
    dt_t = dt.T
    xs_t = xs.T

    for g in range(SSM_GROUPS):
        b_g = b_all[:, g * SSM_D_STATE:(g + 1) * SSM_D_STATE]
        c_g = c_all[:, g * SSM_D_STATE:(g + 1) * SSM_D_STATE]
        cb = _dot_nt(c_g, b_g)
        for hh in range(heads_per_group):
            h = g * heads_per_group + hh
            col = acum[:, h:h + 1]
            arow = acum_t[h:h + 1, :]
            dtrow = dt_t[h:h + 1, :]
            alast = acum_t[h:h + 1, lc - 1:lc]
            decay = jnp.where(tril, jnp.exp(jnp.where(tril, col - arow, 0.0)), 0.0)
            mh = (cb * decay * dtrow).astype(BF16)
            y_diag = _dot(mh, xs[:, h * p:(h + 1) * p].astype(BF16))
            s_h = s_ref[h * p:(h + 1) * p, :]
            y_off = _dot_nt(c_g, s_h.astype(BF16)) * jnp.exp(col)
            yh_ref[:, h * p:(h + 1) * p] = y_diag + y_off
            w_row = dtrow * jnp.exp(alast - arow)
            x_t = (xs_t[h * p:(h + 1) * p, :] * w_row).astype(BF16)
            s_ref[h * p:(h + 1) * p, :] = s_h * jnp.exp(alast) + _dot(x_t, b_g)

    y = yh_ref[...] + dsk_ref[...] * xs
    y = y * _silu(z_ref[...])
    gw = d_inner // SSM_GROUPS
    for g in range(SSM_GROUPS):
        yg = y[:, g * gw:(g + 1) * gw]
        ms = jnp.mean(yg * yg, axis=1, keepdims=True)
        y_ref[:, g * gw:(g + 1) * gw] = yg * lax.rsqrt(ms + RMS_EPS) * nw_ref[:, g * gw:(g + 1) * gw]


def _ssd(xbc, z_src, dt, conv_prev, ssm_prev, conv_w, conv_b, dt_bias, a_log, d_skip, norm_w, *, l_valid):
    b, l, conv_dim = xbc.shape
    d_inner = norm_w.shape[1]
    nc = l // SSD_CHUNK
    tok = lambda w: pl.BlockSpec((None, SSD_CHUNK, w), lambda bi, c: (bi, c, 0))
    per_b = lambda s: pl.BlockSpec((None,) + s, lambda bi, c: (bi, 0, 0))
    par = lambda a: pl.BlockSpec(a.shape, lambda bi, c: (0, 0))
    kern = functools.partial(_ssd_kernel, l_valid=l_valid)
    return pl.pallas_call(
        kern,
        grid=(b, nc),
        in_specs=[tok(conv_dim), tok(d_inner), tok(LANES),
                  per_b((SUBLANES, conv_dim)), per_b(ssm_prev.shape[1:]),
                  par(conv_w), par(conv_b), par(dt_bias), par(a_log), par(d_skip), par(norm_w)],
        out_specs=[tok(d_inner), per_b((SUBLANES, conv_dim)), per_b(ssm_prev.shape[1:])],
        out_shape=[jax.ShapeDtypeStruct((b, l, d_inner), F32),
                   jax.ShapeDtypeStruct((b, SUBLANES, conv_dim), F32),
                   jax.ShapeDtypeStruct(ssm_prev.shape, F32)],
        scratch_shapes=[pltpu.VMEM((SSD_CHUNK + 2 * SUBLANES, conv_dim), F32),
                        pltpu.VMEM((SSD_CHUNK, d_inner), F32)],
        compiler_params=_cparams("parallel", "arbitrary"),
        name="ssd",
    )(xbc, z_src, dt, conv_prev, ssm_prev, conv_w, conv_b, dt_bias, a_log, d_skip, norm_w)


def _layer_norm(x, g, b):
    mu = jnp.mean(x, axis=-1, keepdims=True)
    xc = x - mu
    var = jnp.mean(xc * xc, axis=-1, keepdims=True)
    return xc * lax.rsqrt(var + LN_EPS) * g + b


def _mix_kernel(x_ref, attn_ref, yssm_ref, gate_ref, bg_ref, wa_ref, wm_ref, wo_ref, g_ref, b_ref, o_ref,
                *, alpha):
    d = x_ref.shape[1]
    ya = _dot(attn_ref[...].astype(BF16), wa_ref[...])
    ym = _dot(yssm_ref[...].astype(BF16), wm_ref[...])
    gates = gate_ref[...] + bg_ref[...]
    gates = 1.0 / (1.0 + jnp.exp(-gates))
    mixed = gates[:, :d] * ya + gates[:, d:] * ym
    out = _dot(mixed.astype(BF16), wo_ref[...])
    o_ref[...] = _layer_norm(alpha * x_ref[...] + out, g_ref[...], b_ref[...])


def _mix(x, attn, y_ssm, zg, b_gate, w_a, w_m, w_o, ln_g, ln_b, *, alpha, tm):
    m, d = x.shape
    gate_block = zg.shape[1] // (N_BRANCHES * d) - 1
    row = lambda w: pl.BlockSpec((tm, w), lambda i: (i, 0))
    par = lambda a: pl.BlockSpec(a.shape, lambda i: (0, 0))
    return pl.pallas_call(
        functools.partial(_mix_kernel, alpha=alpha),
        grid=(m // tm,),
        in_specs=[row(d), row(attn.shape[1]), row(y_ssm.shape[1]),
                  pl.BlockSpec((tm, N_BRANCHES * d), lambda i: (i, gate_block)),
                  par(b_gate), par(w_a), par(w_m), par(w_o), par(ln_g), par(ln_b)],
        out_specs=row(d),
        out_shape=jax.ShapeDtypeStruct((m, d), F32),
        compiler_params=_cparams("parallel"),
        name="mix",
    )(x, attn, y_ssm, zg, b_gate, w_a, w_m, w_o, ln_g, ln_b)


def _top_values(x, n, with_rank=False):
    vals = []
    rank = jnp.full(x.shape, float(n), F32)
    for i in range(n):
        m = jnp.max(x, axis=0, keepdims=True)
        vals.append(m)
        hit = x == m
        if with_rank:
            rank = jnp.where(hit, float(i), rank)
        x = jnp.where(hit, -jnp.inf, x)
    return (vals, rank) if with_rank else vals


def _gelu(x):
    return 0.5 * x * (1.0 + lax.erf(x * (2.0 ** -0.5)))


def _peer_kernel(h_ref, wqt_ref, k1_ref, k2_ref, u_ref, v_ref, g_ref, b_ref, o_ref,
                 ht_scr, nsel_scr, rank_scr, e1_scr, e2_scr, act_scr, wa_scr, acc_scr, *, alpha, i1_per_step):
    s = pl.program_id(1)
    n_blocks = pl.num_programs(1) - 1
    half = PEER_QDIM // 2
    nk = PEER_NKEYS

    @pl.when(s == 0)
    def _():
        ht = h_ref[...].T.astype(BF16)
        ht_scr[...] = ht
        k1 = k1_ref[...].astype(BF16)
        k2 = k2_ref[...].astype(BF16)
        for hd in range(PEER_HEADS):
            qt = _dot(wqt_ref[hd * PEER_QDIM:(hd + 1) * PEER_QDIM, :], ht).astype(BF16)
            s1 = _dot(k1, qt[:half])
            s2 = _dot(k2, qt[half:])
            v1 = _top_values(s1, PEER_TOPK)
            v2, rank2 = _top_values(s2, PEER_TOPK, with_rank=True)
            cand = jnp.concatenate(
                [v1[i] + v2[jj] for i in range(PEER_TOPK) for jj in range(PEER_TOPK // (i + 1))], axis=0)
            top = _top_values(cand, PEER_TOPK)
            tau = top[PEER_TOPK - 1]
            zsum = sum(jnp.exp(t - top[0]) for t in top)
            n_sel = jnp.zeros(s1.shape, F32)
            for jj in range(PEER_TOPK):
                n_sel = jnp.where(s1 + v2[jj] >= tau, float(jj + 1), n_sel)
            nsel_scr[hd] = n_sel
            rank_scr[hd] = rank2.astype(BF16)
            e1_scr[hd] = jnp.exp(s1 - v1[0])
            e2_scr[hd] = (jnp.exp(s2 - v2[0]) / zsum).astype(BF16)
        acc_scr[...] = jnp.zeros(acc_scr.shape, F32)
        act_scr[...] = jnp.zeros(act_scr.shape, F32)

    prev = jnp.maximum(s - 1, 0)
    tt = acc_scr.shape[0]
    n_cols = tt // LANES
    dk = ht_scr.shape[0] // n_cols
    rb = 4 * SUBLANES

    def u_chunk(c):
        return _dot(u_ref[:, c * dk:(c + 1) * dk], ht_scr[c * dk:(c + 1) * dk, :])

    def i1_rows(scr):
        return [[scr[hd, pl.ds(prev * i1_per_step + ii, 1), :].astype(BF16) for ii in range(i1_per_step)]
                for hd in range(PEER_HEADS)]

    nsel_rows = i1_rows(nsel_scr)
    cw_rows = i1_rows(e1_scr)

    def gate_cols(c):
        cols = slice(c * LANES, (c + 1) * LANES)
        for r0 in range(0, nk, rb):
            w = [jnp.zeros((rb, LANES), BF16) for _ in range(i1_per_step)]
            for hd in range(PEER_HEADS):
                rk = rank_scr[hd, r0:r0 + rb, cols]
                e2t = e2_scr[hd, r0:r0 + rb, cols]
                for ii in range(i1_per_step):
                    ns = nsel_rows[hd][ii][:, cols]
                    cw = cw_rows[hd][ii][:, cols]
                    w[ii] = w[ii] + jnp.where(rk < ns, e2t * cw, jnp.zeros((), BF16))
            for ii in range(i1_per_step):
                rows = slice(ii * nk + r0, ii * nk + r0 + rb)
                wa_scr[rows, cols] = w[ii] * act_scr[(s + 1) % 2, rows, cols].astype(BF16)

    pre = None
    for c in range(n_cols):
        uc = u_chunk(c)
        pre = uc if pre is None else pre + uc
        gate_cols(c)
    acc_scr[...] += _dot(wa_scr[...].T, v_ref[...])
    act_scr[s % 2] = _gelu(pre)

    @pl.when(s == pl.num_programs(1) - 1)
    def _():
        hblk = h_ref[...]
        o_ref[...] = _layer_norm(alpha * hblk + acc_scr[...], g_ref[...], b_ref[...])


def _peer(h, wq_t, keys1, keys2, u, v, ln_g, ln_b, *, alpha, tt, i1_per_step):
    m, d = h.shape
    n_exp = u.shape[0]
    eb = i1_per_step * PEER_NKEYS
    n_blocks = n_exp // eb
    par = lambda a: pl.BlockSpec(a.shape, lambda i, s: (0, 0))
    scr = lambda dt: pltpu.VMEM((PEER_HEADS, PEER_NKEYS, tt), dt)
    return pl.pallas_call(
        functools.partial(_peer_kernel, alpha=alpha, i1_per_step=i1_per_step),
        grid=(m // tt, n_blocks + 1),
        in_specs=[pl.BlockSpec((tt, d), lambda i, s: (i, 0)), par(wq_t), par(keys1), par(keys2),
                  pl.BlockSpec((eb, d), lambda i, s: (jnp.minimum(s, n_blocks - 1), 0)),
                  pl.BlockSpec((eb, d), lambda i, s: (jnp.maximum(s - 1, 0), 0)),
                  par(ln_g), par(ln_b)],
        out_specs=pl.BlockSpec((tt, d), lambda i, s: (i, 0)),
        out_shape=jax.ShapeDtypeStruct((m, d), F32),
        scratch_shapes=[pltpu.VMEM((d, tt), BF16), scr(F32), scr(BF16), scr(F32), scr(BF16),
                        pltpu.VMEM((2, eb, tt), F32), pltpu.VMEM((eb, tt), BF16), pltpu.VMEM((tt, d), F32)],
        compiler_params=_cparams("parallel", "arbitrary"),
        name="peer",
    )(h, wq_t, keys1, keys2, u, v, ln_g, ln_b)


def _pad_cols(w, width):
    return jnp.pad(w, ((0, 0), (0, width - w.shape[1])))


def _prep_weights(w_in, d_model, d_inner, conv_dim, n_ssm_heads):
    att_w = ATT_HEADS * ATT_HEAD_DIM
    kv_w = ATT_KV_HEADS * ATT_HEAD_DIM
    splits = (att_w, kv_w, kv_w, IDX_HEADS * IDX_DIM, IDX_DIM, IDX_HEADS,
              d_inner, conv_dim, n_ssm_heads, N_BRANCHES * d_model)
    offs = np.cumsum((0,) + splits)
    piece = lambda i: w_in[:, offs[i]:offs[i + 1]]
    w_a = jnp.concatenate([piece(0), piece(1), piece(2), piece(3),
                           _pad_cols(piece(4), LANES), _pad_cols(piece(5), LANES), _pad_cols(piece(8), LANES)],
                          axis=1).astype(BF16)
    w_xbc = piece(7).astype(BF16)
    w_zg = jnp.concatenate([piece(6), piece(9)], axis=1).astype(BF16)
    return w_a, w_xbc, w_zg


def _tiles(m, n_keys):
    kt = min((512, 384, 256), key=lambda t: (-(-n_keys // t) * t, -t))
    return min(m, 512), min(m, 1024), min(m, 512), kt


def _stream(x, pos, past, wts):
    (w_a, w_xbc, w_zg, b_gate, conv_w, conv_b, dt_bias, a_log, d_skip_row, norm_w, w_att, w_ssm, w_out,
     ln1_g, ln1_b, wq_t, keys1, keys2, u_exp, v_exp, ln2_g, ln2_b, alpha) = wts
    b, l, d = x.shape
    m = b * l
    xf = x.reshape(m, d)
    conv_dim = w_xbc.shape[1]
    d_inner = norm_w.shape[1]
    tm, tm_mm, tt, kt = _tiles(m, l if past is None else past[0].shape[1] + l)

    tables = _rope_tables(jnp.asarray(np.tile(pos, max(1, tm // l))))
    q, k, v, qi, ki, wi, dt = _proj_a(xf, w_a, tables, tm)
    xbc = _matmul(xf, w_xbc, tm_mm, 1024)
    zg = _matmul(xf, w_zg, tm_mm, 1024)

    r3 = lambda a: a.reshape(b, l, a.shape[1])
    if past is None:
        k_all, v_all, ki_all = r3(k), r3(v), r3(ki)
        n_valid = l
        k_sel = min(TOPK_MAX, l // 4)
        conv_prev = jnp.zeros((b, SUBLANES, conv_dim), F32)
        ssm_prev = jnp.zeros((b, d_inner, SSM_D_STATE), F32)
    else:
        ck, cv, cki, conv_state, ssm_state = past
        n_past = ck.shape[1]
        n_valid = n_past + l
        n_pad = -n_valid % kt
        cat = lambda c, new: jnp.pad(jnp.concatenate([c.reshape(b, n_past, -1), r3(new)], axis=1),
                                     ((0, 0), (0, n_pad), (0, 0)))
        k_all, v_all, ki_all = cat(ck, k), cat(cv, v), cat(cki, ki)
        k_sel = min(TOPK_MAX, n_valid // 4)
        conv_prev = jnp.pad(conv_state, ((0, 0), (SUBLANES - (CONV_WIDTH - 1), 0), (0, 0)))
        ssm_prev = ssm_state.reshape(b, d_inner, SSM_D_STATE)
    q_pad = -l % LANES
    padq = lambda a: jnp.pad(r3(a), ((0, 0), (0, q_pad), (0, 0)))
    attn = _dsa(padq(q), padq(qi), padq(wi), k_all, v_all, ki_all,
                kt=kt, n_valid=n_valid, q_pos0=int(pos[0]), k_sel=k_sel)[:, :l]

    l_pad = -l % SSD_CHUNK
    padl = lambda a: jnp.pad(r3(a), ((0, 0), (0, l_pad), (0, 0)))
    y_ssm, conv_new, ssm_new = _ssd(padl(xbc), padl(zg), padl(dt), conv_prev, ssm_prev,
                                    conv_w, conv_b, dt_bias, a_log, d_skip_row, norm_w,
                                    l_valid=SSD_CHUNK if l_pad == 0 else l)
    y_ssm = y_ssm[:, :l].reshape(m, d_inner)
    conv_new = conv_new[:, SUBLANES - (CONV_WIDTH - 1):]
    n_ssm_heads = d_inner // SSM_HEAD_DIM
    ssm_new = ssm_new.reshape(b, n_ssm_heads, SSM_HEAD_DIM, SSM_D_STATE)

    h1 = _mix(xf, attn.reshape(m, -1), y_ssm, zg, b_gate, w_att, w_ssm, w_out, ln1_g, ln1_b,
              alpha=alpha, tm=tm)
    y = _peer(h1, wq_t, keys1, keys2, u_exp, v_exp, ln2_g, ln2_b, alpha=alpha, tt=tt, i1_per_step=4)
    state = (r3(k).reshape(b, l, ATT_KV_HEADS, ATT_HEAD_DIM), r3(v).reshape(b, l, ATT_KV_HEADS, ATT_HEAD_DIM),
             r3(ki), conv_new, ssm_new)
    return y.reshape(b, l, d), state


def kernel(x_prompt, x_sample, cache_k, cache_v, cache_kidx, state_conv, state_ssm, w_in, b_gate, conv_w, conv_b, dt_bias, a_log, d_skip, ssm_norm_w, w_attn_br, w_ssm_br, w_out, ln1_g, ln1_b, peer_wq, peer_keys1, peer_keys2, peer_u, peer_v, ln2_g, ln2_b):
    depth = w_in.shape[0]
    d_model = x_prompt.shape[2]
    d_inner = ssm_norm_w.shape[1]
    conv_dim = conv_w.shape[2]
    n_ssm_heads = a_log.shape[1]
    alpha = (2.0 * depth) ** 0.25
    pos_p = np.arange(x_prompt.shape[1])
    pos_s = cache_k.shape[2] + np.arange(x_sample.shape[1])

    hp, hs = x_prompt, x_sample
    new_p, new_s = [], []
    for l in range(depth):
        w_a, w_xbc, w_zg = _prep_weights(w_in[l], d_model, d_inner, conv_dim, n_ssm_heads)
        row = lambda a: a.reshape(1, -1)
        wts = (w_a, w_xbc, w_zg, row(b_gate[l]),
               jnp.pad(conv_w[l], ((0, SUBLANES - CONV_WIDTH), (0, 0))), row(conv_b[l]),
               _pad_cols(row(dt_bias[l]), LANES), _pad_cols(row(a_log[l]), LANES),
               row(jnp.repeat(d_skip[l], SSM_HEAD_DIM)), row(ssm_norm_w[l]),
               w_attn_br[l].astype(BF16), w_ssm_br[l].astype(BF16), w_out[l].astype(BF16),
               row(ln1_g[l]), row(ln1_b[l]),
               peer_wq[l].T.astype(BF16), peer_keys1[l], peer_keys2[l],
               peer_u[l].astype(BF16), peer_v[l].astype(BF16), row(ln2_g[l]), row(ln2_b[l]), alpha)
        hp, sp = _stream(hp, pos_p, None, wts)
        past = (cache_k[l], cache_v[l], cache_kidx[l], state_conv[l], state_ssm[l])
        hs, ss = _stream(hs, pos_s, past, wts)
        new_p.append(sp)
        new_s.append(ss)

    stack = lambda lst, i: jnp.stack([e[i] for e in lst], axis=0)
    return (hp, hs) + tuple(stack(new_p, i) for i in range(5)) + tuple(stack(new_s, i) for i in range(5))
```

```python
import functools

import jax
import jax.numpy as jnp
import numpy as np
from jax import lax
from jax.experimental import pallas as pl
from jax.experimental.pallas import tpu as pltpu

F32 = jnp.float32
BF16 = jnp.bfloat16

LANES = 128
SUBLANES = 8
VMEM_LIMIT_BYTES = 48 * 1024 * 1024

CHUNK = 64
ATT_HEADS = 16
ATT_HEAD_DIM = 64
ATT_KV_HEADS = 4
ATT_GROUP = ATT_HEADS // ATT_KV_HEADS
IDX_HEADS = 4
IDX_DIM = 64
TOPK_MAX = 256
ROPE_THETA = 500000.0
ROPE_FRACTION = 4
SSM_HEAD_DIM = 64
SSM_GROUPS = 4
SSM_D_STATE = 128
CONV_WIDTH = 4
N_BRANCHES = 2
PEER_HEADS = 8
PEER_NKEYS = 128
PEER_QDIM = 256
PEER_TOPK = 16
LN_EPS = 1e-5
RMS_EPS = 1e-5
NEG_INF = -1e30
SSD_CHUNK = 128
LOG2_E = 1.4426950408889634
INT32_MIN = -(2 ** 31)
CODE_NEG_INFINITY = INT32_MIN + 0x7FFFFF


def _cparams(*sem):
    return pltpu.CompilerParams(dimension_semantics=sem, vmem_limit_bytes=VMEM_LIMIT_BYTES)


def _dot(a, b):
    return jnp.dot(a, b, preferred_element_type=F32)


def _dot_nt(a, b):
    return lax.dot_general(a, b, (((1,), (1,)), ((), ())), preferred_element_type=F32)


def _rope_tables(pos):
    rd = ATT_HEAD_DIM // ROPE_FRACTION
    half = rd // 2
    inv = ROPE_THETA ** (-(jnp.arange(half, dtype=F32) * 2.0) / rd)
    ang = pos.astype(F32)[:, None] * inv[None, :]
    cos, sin = jnp.cos(ang), jnp.sin(ang)
    n = pos.shape[0]
    ones = jnp.ones((n, ATT_HEAD_DIM - rd), F32)
    zeros = jnp.zeros((n, ATT_HEAD_DIM - rd), F32)
    zh = jnp.zeros((n, half), F32)
    cos_h = jnp.concatenate([cos, cos, ones], axis=1)
    sa_h = jnp.concatenate([-sin, zh, zeros], axis=1)
    sb_h = jnp.concatenate([zh, sin, zeros], axis=1)
    rep = LANES // ATT_HEAD_DIM
    return jnp.tile(cos_h, (1, rep)), jnp.tile(sa_h, (1, rep)), jnp.tile(sb_h, (1, rep))


def _proj_a_kernel(x_ref, w_ref, cos_ref, sa_ref, sb_ref,
                   q_ref, k_ref, v_ref, qi_ref, ki_ref, wi_ref, dt_ref):
    x = x_ref[...].astype(BF16)
    cos, sa, sb = cos_ref[...], sa_ref[...], sb_ref[...]
    half = ATT_HEAD_DIM // ROPE_FRACTION // 2

    def rope(t):
        up = pltpu.roll(t, LANES - half, 1)
        down = pltpu.roll(t, half, 1)
        return t * cos + up * sa + down * sb

    def tile(c):
        return _dot(x, w_ref[:, c * LANES:(c + 1) * LANES])

    nq = q_ref.shape[1] // LANES
    nk = k_ref.shape[1] // LANES
    c = 0
    for j in range(nq):
        q_ref[:, j * LANES:(j + 1) * LANES] = rope(tile(c + j))
    c += nq
    for j in range(nk):
        k_ref[:, j * LANES:(j + 1) * LANES] = rope(tile(c + j))
    c += nk
    for j in range(nk):
        v_ref[:, j * LANES:(j + 1) * LANES] = tile(c + j)
    c += nk
    nqi = qi_ref.shape[1] // LANES
    for j in range(nqi):
        qi_ref[:, j * LANES:(j + 1) * LANES] = rope(tile(c + j))
    c += nqi
    ki_ref[...] = rope(tile(c))[:, :IDX_DIM]
    wi_ref[...] = tile(c + 1)
    dt_ref[...] = tile(c + 2)


def _proj_a(x, w_a, tables, tm):
    m, d = x.shape
    att_w = ATT_HEADS * ATT_HEAD_DIM
    kv_w = ATT_KV_HEADS * ATT_HEAD_DIM
    idx_w = IDX_HEADS * IDX_DIM
    tab_blocks = tables[0].shape[0] // tm
    row = lambda i: (i, 0)
    tab = lambda i: (i % tab_blocks, 0)
    full = lambda i: (0, 0)
    out_w = (att_w, kv_w, kv_w, idx_w, IDX_DIM, LANES, LANES)
    return pl.pallas_call(
        _proj_a_kernel,
        grid=(m // tm,),
        in_specs=[pl.BlockSpec((tm, d), row), pl.BlockSpec(w_a.shape, full)]
                 + [pl.BlockSpec((tm, LANES), tab)] * 3,
        out_specs=[pl.BlockSpec((tm, w), row) for w in out_w],
        out_shape=[jax.ShapeDtypeStruct((m, w), F32) for w in out_w],
        compiler_params=_cparams("parallel"),
        name="proj_a",
    )(x, w_a, *tables)


def _matmul_kernel(x_ref, w_ref, o_ref):
    o_ref[...] = _dot(x_ref[...].astype(BF16), w_ref[...])


def _matmul(x, w, tm, tn):
    m, k = x.shape
    n = w.shape[1]
    return pl.pallas_call(
        _matmul_kernel,
        grid=(m // tm, n // tn),
        in_specs=[pl.BlockSpec((tm, k), lambda i, j: (i, 0)), pl.BlockSpec((k, tn), lambda i, j: (0, j))],
        out_specs=pl.BlockSpec((tm, tn), lambda i, j: (i, j)),
        out_shape=jax.ShapeDtypeStruct((m, n), F32),
        compiler_params=_cparams("parallel", "arbitrary"),
        name="matmul",
    )(x, w)


def _tree(op, xs):
    xs = list(xs)
    while len(xs) > 1:
        xs = [op(xs[i], xs[i + 1]) for i in range(0, len(xs) - 1, 2)] + ([xs[-1]] if len(xs) % 2 else [])
    return xs[0]


def _lane_tiles(x):
    return [x[:, c * LANES:(c + 1) * LANES] for c in range(x.shape[1] // LANES)]


def _dsa_kernel(q_ref, qi_ref, wi_ref, k_ref, v_ref, ki_ref, earlier_ref, o_ref,
                score_scr, sel_scr, lg_scr, q_scr, m_scr, l_scr, acc_scr,
                *, kt, nkt, n_valid, q_pos0, k_sel):
    qb = LANES
    j = pl.program_id(1)
    q_first = q_pos0 + j * qb
    nt = jnp.minimum(nkt, (q_first + qb + kt - 1) // kt)

    key_off = lax.broadcasted_iota(jnp.int32, (kt, qb), 0)
    q_chunk = (q_first + lax.broadcasted_iota(jnp.int32, (1, qb), 1)) // CHUNK
    key_limit = jnp.minimum((q_chunk + 1) * CHUNK, n_valid)

    def admissible(t):
        return key_off < key_limit - t * kt

    qi_t = qi_ref[...].T.astype(BF16)
    wi_t = wi_ref[...].T

    def score_tile(t, carry):
        ki_t = ki_ref[pl.ds(pl.multiple_of(t * kt, LANES), kt), :].astype(BF16)
        s = jnp.zeros((kt, qb), F32)
        for h in range(IDX_HEADS):
            lg = _dot(ki_t, qi_t[h * IDX_DIM:(h + 1) * IDX_DIM, :])
            s = s + jnp.maximum(lg, 0.0) * wi_t[h:h + 1, :]
        s = jnp.where(s == 0.0, 0.0, s)
        score_scr[t] = jnp.where(admissible(t), s, NEG_INF)
        return carry

    lax.fori_loop(0, nt, score_tile, 0)

    def count(pred):
        def body(t, acc):
            hit = jnp.where(pred(score_scr[t]), 1.0, 0.0)
            return acc + _tree(jnp.add, [hit[r:r + SUBLANES] for r in range(0, kt, SUBLANES)])
        acc = lax.fori_loop(0, nt, body, jnp.zeros((SUBLANES, qb), F32))
        return jnp.sum(acc, axis=0, keepdims=True)

    kf = jnp.float32(k_sel)

    def decode(code):
        bits = jnp.where(code < 0, code ^ jnp.int32(0x7FFFFFFF), code)
        return lax.bitcast_convert_type(bits, F32)

    def at_least_k(code):
        thr_f = decode(code)
        return jnp.logical_or(count(lambda sc: sc >= thr_f) >= kf, code < CODE_NEG_INFINITY)

    zero = jnp.zeros((1, qb), jnp.int32)
    code = jnp.where(count(lambda sc: sc >= 0.0) >= kf, zero, jnp.int32(INT32_MIN))

    def bit_pass(i, code):
        cand = code | jnp.left_shift(jnp.int32(1), 30 - i)
        return jnp.where(at_least_k(cand), cand, code)

    thr = decode(lax.fori_loop(0, 31, bit_pass, code))

    need = kf - count(lambda sc: sc > thr)
    earlier = earlier_ref[...]

    def select_tile(t, run):
        sc = score_scr[t]
        tie = sc == thr
        eq = jnp.where(tie, 1.0, 0.0)
        rank = _dot(earlier, eq.astype(BF16)) + run
        take = jnp.logical_or(sc > thr, jnp.logical_and(tie, rank < need))
        sel = jnp.where(jnp.logical_and(take, admissible(t)), 1.0, 0.0)
        sel_scr[t] = sel.T
        return run + jnp.sum(eq, axis=0, keepdims=True)

    lax.fori_loop(0, nt, select_tile, jnp.zeros((1, qb), F32))

    scale = ATT_HEAD_DIM ** -0.5 * LOG2_E
    hd = ATT_HEAD_DIM
    rows = ATT_GROUP * qb
    pair = LANES // hd
    for g in range(ATT_KV_HEADS):
        qg = jnp.concatenate(
            [q_ref[:, (g * ATT_GROUP + r) * hd:(g * ATT_GROUP + r + 1) * hd] for r in range(ATT_GROUP)],
            axis=0) * scale
        zeros = jnp.zeros_like(qg)
        parts = [qg if c == g % pair else zeros for c in range(pair)]
        q_scr[g] = jnp.concatenate(parts, axis=1).astype(BF16)
    m_scr[...] = jnp.full(m_scr.shape, NEG_INF, F32)
    l_scr[...] = jnp.zeros(l_scr.shape, F32)
    acc_scr[...] = jnp.zeros(acc_scr.shape, F32)

    def kv_tile(ref, t, g):
        lane0 = (g // pair) * LANES
        return ref[pl.ds(pl.multiple_of(t * kt, LANES), kt), lane0:lane0 + LANES].astype(BF16)

    def logits_tile(t, carry):
        sel = sel_scr[t][None] > 0.0
        for g in range(ATT_KV_HEADS):
            lg = _dot_nt(q_scr[g], kv_tile(k_ref, t, g)).reshape(ATT_GROUP, qb, kt)
            lg = jnp.where(sel, lg, NEG_INF).reshape(rows, kt)
            lg_scr[t, g] = lg
            m_scr[g] = jnp.maximum(m_scr[g], _tree(jnp.maximum, _lane_tiles(lg)))
        return carry

    lax.fori_loop(0, nt, logits_tile, 0)
    for g in range(ATT_KV_HEADS):
        m_scr[g] = jnp.broadcast_to(jnp.max(m_scr[g], axis=1, keepdims=True), (rows, LANES))

    def pv_tile(t, carry):
        for g in range(ATT_KV_HEADS):
            m = m_scr[g]
            ps = [jnp.exp2(lg_c - m) for lg_c in _lane_tiles(lg_scr[t, g])]
            l_scr[g] += _tree(jnp.add, ps)
            acc_scr[g] += _dot(jnp.concatenate(ps, axis=1).astype(BF16), kv_tile(v_ref, t, g))
        return carry

    lax.fori_loop(0, nt, pv_tile, 0)
    for g in range(ATT_KV_HEADS):
        denom = jnp.sum(l_scr[g], axis=1, keepdims=True)
        out = acc_scr[g][:, (g % pair) * hd:(g % pair + 1) * hd] / denom
        for r in range(ATT_GROUP):
            h = g * ATT_GROUP + r
            o_ref[:, h * hd:(h + 1) * hd] = out[r * qb:(r + 1) * qb]


def _dsa(q, qi, wi, k, v, ki, *, kt, n_valid, q_pos0, k_sel):
    b, lq, _ = q.shape
    nk = k.shape[1]
    nkt = nk // kt
    qb = LANES
    qspec = lambda w: pl.BlockSpec((None, qb, w), lambda bi, j: (bi, j, 0))
    kspec = lambda w: pl.BlockSpec((None, nk, w), lambda bi, j: (bi, 0, 0))
    rows = ATT_GROUP * qb
    assert kt >= k_sel, "a query block must see at least k_sel key slots"
    earlier = jnp.tril(jnp.ones((kt, kt), BF16), -1)
    kern = functools.partial(_dsa_kernel, kt=kt, nkt=nkt, n_valid=n_valid, q_pos0=q_pos0, k_sel=k_sel)
    return pl.pallas_call(
        kern,
        grid=(b, lq // qb),
        in_specs=[qspec(q.shape[2]), qspec(qi.shape[2]), qspec(wi.shape[2]),
                  kspec(k.shape[2]), kspec(v.shape[2]), kspec(ki.shape[2]),
                  pl.BlockSpec((kt, kt), lambda bi, j: (0, 0))],
        out_specs=qspec(q.shape[2]),
        out_shape=jax.ShapeDtypeStruct(q.shape, F32),
        scratch_shapes=[
            pltpu.VMEM((nkt, kt, qb), F32),
            pltpu.VMEM((nkt, qb, kt), F32),
            pltpu.VMEM((nkt, ATT_KV_HEADS, rows, kt), F32),
            pltpu.VMEM((ATT_KV_HEADS, rows, LANES), BF16),
            pltpu.VMEM((ATT_KV_HEADS, rows, LANES), F32),
            pltpu.VMEM((ATT_KV_HEADS, rows, LANES), F32),
            pltpu.VMEM((ATT_KV_HEADS, rows, LANES), F32),
        ],
        compiler_params=_cparams("parallel", "arbitrary"),
        name="dsa",
    )(q, qi, wi, k, v, ki, earlier)


def _silu(x):
    return x / (1.0 + jnp.exp(-x))


def _ssd_kernel(xbc_ref, z_ref, dt_ref, cprev_ref, sprev_ref, cw_ref, cb_ref, dtb_ref, alog_ref,
                dsk_ref, nw_ref, y_ref, cnew_ref, s_ref, buf_ref, yh_ref, *, l_valid):
    c = pl.program_id(1)
    lc = SSD_CHUNK
    hist = SUBLANES
    d_inner = y_ref.shape[1]
    n_heads = d_inner // SSM_HEAD_DIM
    gn = SSM_GROUPS * SSM_D_STATE
    heads_per_group = n_heads // SSM_GROUPS
    p = SSM_HEAD_DIM

    @pl.when(c == 0)
    def _():
        buf_ref[0:hist, :] = cprev_ref[...]
        s_ref[...] = sprev_ref[...]

    buf_ref[hist:hist + lc, :] = xbc_ref[...]
    conv = cb_ref[...] + sum(
        buf_ref[hist - (CONV_WIDTH - 1) + t:hist - (CONV_WIDTH - 1) + t + lc, :] * cw_ref[t:t + 1, :]
        for t in range(CONV_WIDTH))
    xbc = _silu(conv)
    cnew_ref[...] = buf_ref[l_valid:l_valid + hist, :]
    buf_ref[0:hist, :] = buf_ref[lc:lc + hist, :]

    xs = xbc[:, :d_inner]
    b_all = xbc[:, d_inner:d_inner + gn].astype(BF16)
    c_all = xbc[:, d_inner + gn:].astype(BF16)

    row = lax.broadcasted_iota(jnp.int32, (lc, LANES), 0)
    pre = dt_ref[...] + dtb_ref[...]
    dt = jnp.maximum(pre, 0.0) + jnp.log1p(jnp.exp(-jnp.abs(pre)))
    dt = jnp.where(row < l_valid, dt, 0.0)
    a = dt * (-jnp.exp(alog_ref[...]))
    ta = lax.broadcasted_iota(jnp.int32, (lc, lc), 0)
    tb = lax.broadcasted_iota(jnp.int32, (lc, lc), 1)
    tril = ta >= tb
    acum = jnp.dot(jnp.where(tril, 1.0, 0.0), a, preferred_element_type=F32,
                   precision=lax.Precision.HIGHEST)
    acum_t = acum.T
    dt_t = dt.T
    xs_t = xs.T

    for g in range(SSM_GROUPS):
        b_g = b_all[:, g * SSM_D_STATE:(g + 1) * SSM_D_STATE]
        c_g = c_all[:, g * SSM_D_STATE:(g + 1) * SSM_D_STATE]
        cb = _dot_nt(c_g, b_g)
        for hh in range(heads_per_group):
            h = g * heads_per_group + hh
            col = acum[:, h:h + 1]
            arow = acum_t[h:h + 1, :]
            dtrow = dt_t[h:h + 1, :]
            alast = acum_t[h:h + 1, lc - 1:lc]
            decay = jnp.where(tril, jnp.exp(jnp.where(tril, col - arow, 0.0)), 0.0)
            mh = (cb * decay * dtrow).astype(BF16)
            y_diag = _dot(mh, xs[:, h * p:(h + 1) * p].astype(BF16))
            s_h = s_ref[h * p:(h + 1) * p, :]
            y_off = _dot_nt(c_g, s_h.astype(BF16)) * jnp.exp(col)
            yh_ref[:, h * p:(h + 1) * p] = y_diag + y_off
            w_row = dtrow * jnp.exp(alast - arow)
            x_t = (xs_t[h * p:(h + 1) * p, :] * w_row).astype(BF16)
            s_ref[h * p:(h + 1) * p, :] = s_h * jnp.exp(alast) + _dot(x_t, b_g)

    y = yh_ref[...] + dsk_ref[...] * xs
    y = y * _silu(z_ref[...])
    gw = d_inner // SSM_GROUPS
    for g in range(SSM_GROUPS):
        yg = y[:, g * gw:(g + 1) * gw]
        ms = jnp.mean(yg * yg, axis=1, keepdims=True)
        y_ref[:, g * gw:(g + 1) * gw] = yg * lax.rsqrt(ms + RMS_EPS) * nw_ref[:, g * gw:(g + 1) * gw]


def _ssd(xbc, z_src, dt, conv_prev, ssm_prev, conv_w, conv_b, dt_bias, a_log, d_skip, norm_w, *, l_valid):
    b, l, conv_dim = xbc.shape
    d_inner = norm_w.shape[1]
    nc = l // SSD_CHUNK
    tok = lambda w: pl.BlockSpec((None, SSD_CHUNK, w), lambda bi, c: (bi, c, 0))
    per_b = lambda s: pl.BlockSpec((None,) + s, lambda bi, c: (bi, 0, 0))
    par = lambda a: pl.BlockSpec(a.shape, lambda bi, c: (0, 0))
    kern = functools.partial(_ssd_kernel, l_valid=l_valid)
    return pl.pallas_call(
        kern,
        grid=(b, nc),
        in_specs=[tok(conv_dim), tok(d_inner), tok(LANES),
                  per_b((SUBLANES, conv_dim)), per_b(ssm_prev.shape[1:]),
                  par(conv_w), par(conv_b), par(dt_bias), par(a_log), par(d_skip), par(norm_w)],
        out_specs=[tok(d_inner), per_b((SUBLANES, conv_dim)), per_b(ssm_prev.shape[1:])],
        out_shape=[jax.ShapeDtypeStruct((b, l, d_inner), F32),
                   jax.ShapeDtypeStruct((b, SUBLANES, conv_dim), F32),
                   jax.ShapeDtypeStruct(ssm_prev.shape, F32)],
        scratch_shapes=[pltpu.VMEM((SSD_CHUNK + 2 * SUBLANES, conv_dim), F32),
                        pltpu.VMEM((SSD_CHUNK, d_inner), F32)],
        compiler_params=_cparams("parallel", "arbitrary"),
        name="ssd",
    )(xbc, z_src, dt, conv_prev, ssm_prev, conv_w, conv_b, dt_bias, a_log, d_skip, norm_w)


def _layer_norm(x, g, b):
    mu = jnp.mean(x, axis=-1, keepdims=True)
    xc = x - mu
    var = jnp.mean(xc * xc, axis=-1, keepdims=True)
    return xc * lax.rsqrt(var + LN_EPS) * g + b


def _mix_kernel(x_ref, attn_ref, yssm_ref, wg_ref, bg_ref, wa_ref, wm_ref, wo_ref, g_ref, b_ref, o_ref,
                *, alpha):
    d = x_ref.shape[1]
    x = x_ref[...]
    ya = _dot(attn_ref[...].astype(BF16), wa_ref[...])
    ym = _dot(yssm_ref[...].astype(BF16), wm_ref[...])
    gates = _dot(x.astype(BF16), wg_ref[...]) + bg_ref[...]
    gates = 1.0 / (1.0 + jnp.exp(-gates))
    mixed = gates[:, :d] * ya + gates[:, d:] * ym
    out = _dot(mixed.astype(BF16), wo_ref[...])
    o_ref[...] = _layer_norm(alpha * x + out, g_ref[...], b_ref[...])


def _mix(x, attn, y_ssm, w_gate, b_gate, w_a, w_m, w_o, ln_g, ln_b, *, alpha, tm):
    m, d = x.shape
    row = lambda w: pl.BlockSpec((tm, w), lambda i: (i, 0))
    par = lambda a: pl.BlockSpec(a.shape, lambda i: (0, 0))
    return pl.pallas_call(
        functools.partial(_mix_kernel, alpha=alpha),
        grid=(m // tm,),
        in_specs=[row(d), row(attn.shape[1]), row(y_ssm.shape[1]),
                  par(w_gate), par(b_gate), par(w_a), par(w_m), par(w_o), par(ln_g), par(ln_b)],
        out_specs=row(d),
        out_shape=jax.ShapeDtypeStruct((m, d), F32),
        compiler_params=_cparams("parallel"),
        name="mix",
    )(x, attn, y_ssm, w_gate, b_gate, w_a, w_m, w_o, ln_g, ln_b)


def _top_values(x, n, with_rank=False):
    vals = []
    rank = jnp.full(x.shape, float(n), F32)
    for i in range(n):
        m = jnp.max(x, axis=0, keepdims=True)
        vals.append(m)
        hit = x == m
        if with_rank:
            rank = jnp.where(hit, float(i), rank)
        x = jnp.where(hit, -jnp.inf, x)
    return (vals, rank) if with_rank else vals


def _gelu(x):
    return 0.5 * x * (1.0 + lax.erf(x * (2.0 ** -0.5)))


def _peer_kernel(h_ref, wqt_ref, k1_ref, k2_ref, u_ref, vt_ref, g_ref, b_ref, o_ref,
                 ht_scr, nsel_scr, rank_scr, e1_scr, e2_scr, act_scr, wa_scr, acc_scr, *, alpha, i1_per_step):
    s = pl.program_id(1)
    n_blocks = pl.num_programs(1) - 1
    half = PEER_QDIM // 2
    nk = PEER_NKEYS

    @pl.when(s == 0)
    def _():
        ht = h_ref[...].T.astype(BF16)
        ht_scr[...] = ht
        k1 = k1_ref[...].astype(BF16)
        k2 = k2_ref[...].astype(BF16)
        for hd in range(PEER_HEADS):
            qt = _dot(wqt_ref[hd * PEER_QDIM:(hd + 1) * PEER_QDIM, :], ht).astype(BF16)
            s1 = _dot(k1, qt[:half])
            s2 = _dot(k2, qt[half:])
            v1 = _top_values(s1, PEER_TOPK)
            v2, rank2 = _top_values(s2, PEER_TOPK, with_rank=True)
            cand = jnp.concatenate(
                [v1[i] + v2[jj] for i in range(PEER_TOPK) for jj in range(PEER_TOPK // (i + 1))], axis=0)
            top = _top_values(cand, PEER_TOPK)
            tau = top[PEER_TOPK - 1]
            zsum = sum(jnp.exp(t - top[0]) for t in top)
            n_sel = jnp.zeros(s1.shape, F32)
            for jj in range(PEER_TOPK):
                n_sel = jnp.where(s1 + v2[jj] >= tau, float(jj + 1), n_sel)
            nsel_scr[hd] = n_sel
            rank_scr[hd] = rank2.astype(BF16)
            e1_scr[hd] = jnp.exp(s1 - v1[0])
            e2_scr[hd] = (jnp.exp(s2 - v2[0]) / zsum).astype(BF16)
        acc_scr[...] = jnp.zeros(acc_scr.shape, F32)
        act_scr[...] = jnp.zeros(act_scr.shape, F32)

    prev = jnp.maximum(s - 1, 0)
    tt = acc_scr.shape[1]
    n_cols = tt // LANES
    dk = ht_scr.shape[0] // n_cols
    rb = 4 * SUBLANES

    def u_chunk(c):
        return _dot(u_ref[:, c * dk:(c + 1) * dk], ht_scr[c * dk:(c + 1) * dk, :])

    def i1_rows(scr):
        return [[scr[hd, pl.ds(prev * i1_per_step + ii, 1), :].astype(BF16) for ii in range(i1_per_step)]
                for hd in range(PEER_HEADS)]

    nsel_rows = i1_rows(nsel_scr)
    cw_rows = i1_rows(e1_scr)

    def gate_cols(c):
        cols = slice(c * LANES, (c + 1) * LANES)
        for r0 in range(0, nk, rb):
            w = [jnp.zeros((rb, LANES), BF16) for _ in range(i1_per_step)]
            for hd in range(PEER_HEADS):
                rk = rank_scr[hd, r0:r0 + rb, cols]
                e2t = e2_scr[hd, r0:r0 + rb, cols]
                for ii in range(i1_per_step):
                    ns = nsel_rows[hd][ii][:, cols]
                    cw = cw_rows[hd][ii][:, cols]
                    w[ii] = w[ii] + jnp.where(rk < ns, e2t * cw, jnp.zeros((), BF16))
            for ii in range(i1_per_step):
                rows = slice(ii * nk + r0, ii * nk + r0 + rb)
                wa_scr[rows, cols] = w[ii] * act_scr[(s + 1) % 2, rows, cols].astype(BF16)

    def second(c0, c1):
        cols = slice(c0 * LANES, c1 * LANES)
        acc_scr[:, cols] += _dot(vt_ref[...], wa_scr[:, cols])

    pre = None
    group = 2 if n_cols % 2 == 0 else 1
    for c in range(n_cols):
        uc = u_chunk(c)
        pre = uc if pre is None else pre + uc
        gate_cols(c)
        if (c + 1) % group == 0:
            second(c + 1 - group, c + 1)
    act_scr[s % 2] = _gelu(pre)

    @pl.when(s == pl.num_programs(1) - 1)
    def _():
        hblk = h_ref[...]
        o_ref[...] = _layer_norm(alpha * hblk + acc_scr[...].T, g_ref[...], b_ref[...])


def _peer(h, wq_t, keys1, keys2, u, v_t, ln_g, ln_b, *, alpha, tt, i1_per_step):
    m, d = h.shape
    n_exp = u.shape[0]
    eb = i1_per_step * PEER_NKEYS
    n_blocks = n_exp // eb
    par = lambda a: pl.BlockSpec(a.shape, lambda i, s: (0, 0))
    scr = lambda dt: pltpu.VMEM((PEER_HEADS, PEER_NKEYS, tt), dt)
    return pl.pallas_call(
        functools.partial(_peer_kernel, alpha=alpha, i1_per_step=i1_per_step),
        grid=(m // tt, n_blocks + 1),
        in_specs=[pl.BlockSpec((tt, d), lambda i, s: (i, 0)), par(wq_t), par(keys1), par(keys2),
                  pl.BlockSpec((eb, d), lambda i, s: (jnp.minimum(s, n_blocks - 1), 0)),
                  pl.BlockSpec((d, eb), lambda i, s: (0, jnp.maximum(s - 1, 0))),
                  par(ln_g), par(ln_b)],
        out_specs=pl.BlockSpec((tt, d), lambda i, s: (i, 0)),
        out_shape=jax.ShapeDtypeStruct((m, d), F32),
        scratch_shapes=[pltpu.VMEM((d, tt), BF16), scr(F32), scr(BF16), scr(F32), scr(BF16),
                        pltpu.VMEM((2, eb, tt), F32), pltpu.VMEM((eb, tt), BF16), pltpu.VMEM((d, tt), F32)],
        compiler_params=_cparams("parallel", "arbitrary"),
        name="peer",
    )(h, wq_t, keys1, keys2, u, v_t, ln_g, ln_b)


def _pad_cols(w, width):
    return jnp.pad(w, ((0, 0), (0, width - w.shape[1])))


def _split_w_in_kernel(w_ref, a_ref, xbc_ref, z_ref, g_ref, *, offs):
    lead = offs[5]
    wi_at = -(-lead // LANES) * LANES
    dt_at = wi_at + LANES
    a_ref[...] = jnp.zeros(a_ref.shape, BF16)
    a_ref[:, 0:lead] = w_ref[:, 0:lead].astype(BF16)
    a_ref[:, wi_at:wi_at + offs[6] - offs[5]] = w_ref[:, offs[5]:offs[6]].astype(BF16)
    a_ref[:, dt_at:dt_at + offs[9] - offs[8]] = w_ref[:, offs[8]:offs[9]].astype(BF16)
    z_ref[...] = w_ref[:, offs[6]:offs[7]].astype(BF16)
    xbc_ref[...] = w_ref[:, offs[7]:offs[8]].astype(BF16)
    g_ref[...] = w_ref[:, offs[9]:offs[10]].astype(BF16)


def _split_w_in(w_in, d_model, d_inner, conv_dim, n_ssm_heads):
    att_w = ATT_HEADS * ATT_HEAD_DIM
    kv_w = ATT_KV_HEADS * ATT_HEAD_DIM
    splits = (att_w, kv_w, kv_w, IDX_HEADS * IDX_DIM, IDX_DIM, IDX_HEADS,
              d_inner, conv_dim, n_ssm_heads, N_BRANCHES * d_model)
    offs = tuple(int(o) for o in np.cumsum((0,) + splits))
    d, n = w_in.shape
    a_w = -(-offs[5] // LANES) * LANES + 2 * LANES
    widths = (a_w, conv_dim, d_inner, N_BRANCHES * d_model)
    tr = LANES
    return pl.pallas_call(
        functools.partial(_split_w_in_kernel, offs=offs),
        grid=(d // tr,),
        in_specs=[pl.BlockSpec((tr, n), lambda i: (i, 0))],
        out_specs=[pl.BlockSpec((tr, w), lambda i: (i, 0)) for w in widths],
        out_shape=[jax.ShapeDtypeStruct((d, w), BF16) for w in widths],
        compiler_params=_cparams("parallel"),
        name="split_w_in",
    )(w_in)


def _tiles(m, n_keys):
    kt = min((512, 384, 256), key=lambda t: (-(-n_keys // t) * t, -t))
    return min(m, 512), min(m, 1024), min(m, 512), kt


def _stream(x, pos, past, wts):
    (w_a, w_xbc, w_z, w_gate, b_gate, conv_w, conv_b, dt_bias, a_log, d_skip_row, norm_w, w_att, w_ssm, w_out,
     ln1_g, ln1_b, wq_t, keys1, keys2, u_exp, v_exp, ln2_g, ln2_b, alpha) = wts
    b, l, d = x.shape
    m = b * l
    xf = x.reshape(m, d)
    conv_dim = w_xbc.shape[1]
    d_inner = norm_w.shape[1]
    tm, tm_mm, tt, kt = _tiles(m, l if past is None else past[0].shape[1] + l)

    tables = _rope_tables(jnp.asarray(np.tile(pos, max(1, tm // l))))
    q, k, v, qi, ki, wi, dt = _proj_a(xf, w_a, tables, tm)
    xbc = _matmul(xf, w_xbc, tm_mm, 1024)
    z = _matmul(xf, w_z, tm_mm, 1024)

    r3 = lambda a: a.reshape(b, l, a.shape[1])
    if past is None:
        k_all, v_all, ki_all = r3(k), r3(v), r3(ki)
        n_valid = l
        k_sel = min(TOPK_MAX, l // 4)
        conv_prev = jnp.zeros((b, SUBLANES, conv_dim), F32)
        ssm_prev = jnp.zeros((b, d_inner, SSM_D_STATE), F32)
    else:
        ck, cv, cki, conv_state, ssm_state = past
        n_past = ck.shape[1]
        n_valid = n_past + l
        n_pad = -n_valid % kt
        cat = lambda c, new: jnp.pad(jnp.concatenate([c.reshape(b, n_past, -1), r3(new)], axis=1),
                                     ((0, 0), (0, n_pad), (0, 0)))
        k_all, v_all, ki_all = cat(ck, k), cat(cv, v), cat(cki, ki)
        k_sel = min(TOPK_MAX, n_valid // 4)
        conv_prev = jnp.pad(conv_state, ((0, 0), (SUBLANES - (CONV_WIDTH - 1), 0), (0, 0)))
        ssm_prev = ssm_state.reshape(b, d_inner, SSM_D_STATE)
    q_pad = -l % LANES
    padq = lambda a: jnp.pad(r3(a), ((0, 0), (0, q_pad), (0, 0)))
    attn = _dsa(padq(q), padq(qi), padq(wi), k_all, v_all, ki_all,
                kt=kt, n_valid=n_valid, q_pos0=int(pos[0]), k_sel=k_sel)[:, :l]

    l_pad = -l % SSD_CHUNK
    padl = lambda a: jnp.pad(r3(a), ((0, 0), (0, l_pad), (0, 0)))
    y_ssm, conv_new, ssm_new = _ssd(padl(xbc), padl(z), padl(dt), conv_prev, ssm_prev,
                                    conv_w, conv_b, dt_bias, a_log, d_skip_row, norm_w,
                                    l_valid=SSD_CHUNK if l_pad == 0 else l)
    y_ssm = y_ssm[:, :l].reshape(m, d_inner)
    conv_new = conv_new[:, SUBLANES - (CONV_WIDTH - 1):]
    n_ssm_heads = d_inner // SSM_HEAD_DIM
    ssm_new = ssm_new.reshape(b, n_ssm_heads, SSM_HEAD_DIM, SSM_D_STATE)

    h1 = _mix(xf, attn.reshape(m, -1), y_ssm, w_gate, b_gate, w_att, w_ssm, w_out, ln1_g, ln1_b,
              alpha=alpha, tm=tm)
    y = _peer(h1, wq_t, keys1, keys2, u_exp, v_exp, ln2_g, ln2_b, alpha=alpha, tt=tt, i1_per_step=4)
    state = (r3(k).reshape(b, l, ATT_KV_HEADS, ATT_HEAD_DIM), r3(v).reshape(b, l, ATT_KV_HEADS, ATT_HEAD_DIM),
             r3(ki), conv_new, ssm_new)
    return y.reshape(b, l, d), state


def kernel(x_prompt, x_sample, cache_k, cache_v, cache_kidx, state_conv, state_ssm, w_in, b_gate, conv_w, conv_b, dt_bias, a_log, d_skip, ssm_norm_w, w_attn_br, w_ssm_br, w_out, ln1_g, ln1_b, peer_wq, peer_keys1, peer_keys2, peer_u, peer_v, ln2_g, ln2_b):
    depth = w_in.shape[0]
    d_model = x_prompt.shape[2]
    d_inner = ssm_norm_w.shape[1]
    conv_dim = conv_w.shape[2]
    n_ssm_heads = a_log.shape[1]
    alpha = (2.0 * depth) ** 0.25
    pos_p = np.arange(x_prompt.shape[1])
    pos_s = cache_k.shape[2] + np.arange(x_sample.shape[1])

    hp, hs = x_prompt, x_sample
    new_p, new_s = [], []
    for l in range(depth):
        w_a, w_xbc, w_z, w_gate = _split_w_in(w_in[l], d_model, d_inner, conv_dim, n_ssm_heads)
        row = lambda a: a.reshape(1, -1)
        wts = (w_a, w_xbc, w_z, w_gate, row(b_gate[l]),
               jnp.pad(conv_w[l], ((0, SUBLANES - CONV_WIDTH), (0, 0))), row(conv_b[l]),
               _pad_cols(row(dt_bias[l]), LANES), _pad_cols(row(a_log[l]), LANES),
               row(jnp.repeat(d_skip[l], SSM_HEAD_DIM)), row(ssm_norm_w[l]),
               w_attn_br[l].astype(BF16), w_ssm_br[l].astype(BF16), w_out[l].astype(BF16),
               row(ln1_g[l]), row(ln1_b[l]),
               peer_wq[l].T.astype(BF16), peer_keys1[l], peer_keys2[l],
               peer_u[l].astype(BF16), peer_v[l].astype(BF16).T, row(ln2_g[l]), row(ln2_b[l]), alpha)
        hp, sp = _stream(hp, pos_p, None, wts)
        past = (cache_k[l], cache_v[l], cache_kidx[l], state_conv[l], state_ssm[l])
        hs, ss = _stream(hs, pos_s, past, wts)
        new_p.append(sp)
        new_s.append(ss)

    stack = lambda lst, i: jnp.stack([e[i] for e in lst], axis=0)
    return (hp, hs) + tuple(stack(new_p, i) for i in range(5)) + tuple(stack(new_s, i) for i in range(5))
```

```python
import functools

import jax
import jax.numpy as jnp
import numpy as np
from jax import lax
from jax.experimental import pallas as pl
from jax.experimental.pallas import tpu as pltpu

F32 = jnp.float32
BF16 = jnp.bfloat16

LANES = 128
SUBLANES = 8
VMEM_LIMIT_BYTES = 48 * 1024 * 1024

CHUNK = 64
ATT_HEADS = 16
ATT_HEAD_DIM = 64
ATT_KV_HEADS = 4
ATT_GROUP = ATT_HEADS // ATT_KV_HEADS
IDX_HEADS = 4
IDX_DIM = 64
TOPK_MAX = 256
ROPE_THETA = 500000.0
ROPE_FRACTION = 4
SSM_HEAD_DIM = 64
SSM_GROUPS = 4
SSM_D_STATE = 128
CONV_WIDTH = 4
N_BRANCHES = 2
PEER_HEADS = 8
PEER_NKEYS = 128
PEER_QDIM = 256
PEER_TOPK = 16
LN_EPS = 1e-5
RMS_EPS = 1e-5
NEG_INF = -1e30
SSD_CHUNK = 128
LOG2_E = 1.4426950408889634
INT32_MIN = -(2 ** 31)
CODE_NEG_INFINITY = INT32_MIN + 0x7FFFFF


def _cparams(*sem):
    return pltpu.CompilerParams(dimension_semantics=sem, vmem_limit_bytes=VMEM_LIMIT_BYTES)


def _dot(a, b):
    return jnp.dot(a, b, preferred_element_type=F32)


def _dot_nt(a, b):
    return lax.dot_general(a, b, (((1,), (1,)), ((), ())), preferred_element_type=F32)


def _rope_tables(pos):
    rd = ATT_HEAD_DIM // ROPE_FRACTION
    half = rd // 2
    inv = ROPE_THETA ** (-(jnp.arange(half, dtype=F32) * 2.0) / rd)
    ang = pos.astype(F32)[:, None] * inv[None, :]
    cos, sin = jnp.cos(ang), jnp.sin(ang)
    n = pos.shape[0]
    ones = jnp.ones((n, ATT_HEAD_DIM - rd), F32)
    zeros = jnp.zeros((n, ATT_HEAD_DIM - rd), F32)
    zh = jnp.zeros((n, half), F32)
    cos_h = jnp.concatenate([cos, cos, ones], axis=1)
    sa_h = jnp.concatenate([-sin, zh, zeros], axis=1)
    sb_h = jnp.concatenate([zh, sin, zeros], axis=1)
    rep = LANES // ATT_HEAD_DIM
    return jnp.tile(cos_h, (1, rep)), jnp.tile(sa_h, (1, rep)), jnp.tile(sb_h, (1, rep))


def _proj_a_kernel(x_ref, w_ref, cos_ref, sa_ref, sb_ref,
                   q_ref, k_ref, v_ref, qi_ref, ki_ref, wi_ref, dt_ref):
    x = x_ref[...].astype(BF16)
    cos, sa, sb = cos_ref[...], sa_ref[...], sb_ref[...]
    half = ATT_HEAD_DIM // ROPE_FRACTION // 2

    def rope(t):
        up = pltpu.roll(t, LANES - half, 1)
        down = pltpu.roll(t, half, 1)
        return t * cos + up * sa + down * sb

    def tile(c):
        return _dot(x, w_ref[:, c * LANES:(c + 1) * LANES])

    nq = q_ref.shape[1] // LANES
    nk = k_ref.shape[1] // LANES
    c = 0
    for j in range(nq):
        q_ref[:, j * LANES:(j + 1) * LANES] = rope(tile(c + j))
    c += nq
    for j in range(nk):
        k_ref[:, j * LANES:(j + 1) * LANES] = rope(tile(c + j))
    c += nk
    for j in range(nk):
        v_ref[:, j * LANES:(j + 1) * LANES] = tile(c + j)
    c += nk
    nqi = qi_ref.shape[1] // LANES
    for j in range(nqi):
        qi_ref[:, j * LANES:(j + 1) * LANES] = rope(tile(c + j))
    c += nqi
    ki_ref[...] = rope(tile(c))[:, :IDX_DIM]
    wi_ref[...] = tile(c + 1)
    dt_ref[...] = tile(c + 2)


def _proj_a(x, w_a, tables, tm):
    m, d = x.shape
    att_w = ATT_HEADS * ATT_HEAD_DIM
    kv_w = ATT_KV_HEADS * ATT_HEAD_DIM
    idx_w = IDX_HEADS * IDX_DIM
    tab_blocks = tables[0].shape[0] // tm
    row = lambda i: (i, 0)
    tab = lambda i: (i % tab_blocks, 0)
    full = lambda i: (0, 0)
    out_w = (att_w, kv_w, kv_w, idx_w, IDX_DIM, LANES, LANES)
    return pl.pallas_call(
        _proj_a_kernel,
        grid=(m // tm,),
        in_specs=[pl.BlockSpec((tm, d), row), pl.BlockSpec(w_a.shape, full)]
                 + [pl.BlockSpec((tm, LANES), tab)] * 3,
        out_specs=[pl.BlockSpec((tm, w), row) for w in out_w],
        out_shape=[jax.ShapeDtypeStruct((m, w), F32) for w in out_w],
        compiler_params=_cparams("parallel"),
        name="proj_a",
    )(x, w_a, *tables)


def _matmul_kernel(x_ref, w_ref, o_ref):
    o_ref[...] = _dot(x_ref[...].astype(BF16), w_ref[...])


def _matmul(x, w, tm, tn):
    m, k = x.shape
    n = w.shape[1]
    return pl.pallas_call(
        _matmul_kernel,
        grid=(m // tm, n // tn),
        in_specs=[pl.BlockSpec((tm, k), lambda i, j: (i, 0)), pl.BlockSpec((k, tn), lambda i, j: (0, j))],
        out_specs=pl.BlockSpec((tm, tn), lambda i, j: (i, j)),
        out_shape=jax.ShapeDtypeStruct((m, n), F32),
        compiler_params=_cparams("parallel", "arbitrary"),
        name="matmul",
    )(x, w)


def _tree(op, xs):
    xs = list(xs)
    while len(xs) > 1:
        xs = [op(xs[i], xs[i + 1]) for i in range(0, len(xs) - 1, 2)] + ([xs[-1]] if len(xs) % 2 else [])
    return xs[0]


def _lane_tiles(x):
    return [x[:, c * LANES:(c + 1) * LANES] for c in range(x.shape[1] // LANES)]


def _dsa_kernel(q_ref, qi_ref, wi_ref, k_ref, v_ref, ki_ref, earlier_ref, o_ref,
                score_scr, sel_scr, lg_scr, q_scr, m_scr, l_scr, acc_scr,
                *, kt, nkt, n_valid, q_pos0, k_sel):
    qb = LANES
    j = pl.program_id(1)
    q_first = q_pos0 + j * qb
    nt = jnp.minimum(nkt, (q_first + qb + kt - 1) // kt)

    key_off = lax.broadcasted_iota(jnp.int32, (kt, qb), 0)
    q_chunk = (q_first + lax.broadcasted_iota(jnp.int32, (1, qb), 1)) // CHUNK
    key_limit = jnp.minimum((q_chunk + 1) * CHUNK, n_valid)

    def admissible(t):
        return key_off < key_limit - t * kt

    qi_t = qi_ref[...].T.astype(BF16)
    wi_t = wi_ref[...].T

    def score_tile(t, carry):
        ki_t = ki_ref[pl.ds(pl.multiple_of(t * kt, LANES), kt), :].astype(BF16)
        s = jnp.zeros((kt, qb), F32)
        for h in range(IDX_HEADS):
            lg = _dot(ki_t, qi_t[h * IDX_DIM:(h + 1) * IDX_DIM, :])
            s = s + jnp.maximum(lg, 0.0) * wi_t[h:h + 1, :]
        s = jnp.where(s == 0.0, 0.0, s)
        score_scr[t] = jnp.where(admissible(t), s, NEG_INF)
        return carry

    lax.fori_loop(0, nt, score_tile, 0)

    def count(pred):
        def body(t, acc):
            hit = jnp.where(pred(score_scr[t]), 1.0, 0.0)
            return acc + _tree(jnp.add, [hit[r:r + SUBLANES] for r in range(0, kt, SUBLANES)])
        acc = lax.fori_loop(0, nt, body, jnp.zeros((SUBLANES, qb), F32))
        return jnp.sum(acc, axis=0, keepdims=True)

    kf = jnp.float32(k_sel)

    def decode(code):
        bits = jnp.where(code < 0, code ^ jnp.int32(0x7FFFFFFF), code)
        return lax.bitcast_convert_type(bits, F32)

    def at_least_k(code):
        thr_f = decode(code)
        return jnp.logical_or(count(lambda sc: sc >= thr_f) >= kf, code < CODE_NEG_INFINITY)

    zero = jnp.zeros((1, qb), jnp.int32)
    code = jnp.where(count(lambda sc: sc >= 0.0) >= kf, zero, jnp.int32(INT32_MIN))

    def bit_pass(i, code):
        cand = code | jnp.left_shift(jnp.int32(1), 30 - i)
        return jnp.where(at_least_k(cand), cand, code)

    thr = decode(lax.fori_loop(0, 31, bit_pass, code))

    need = kf - count(lambda sc: sc > thr)
    earlier = earlier_ref[...]

    def select_tile(t, run):
        sc = score_scr[t]
        tie = sc == thr
        eq = jnp.where(tie, 1.0, 0.0)
        rank = _dot(earlier, eq.astype(BF16)) + run
        take = jnp.logical_or(sc > thr, jnp.logical_and(tie, rank < need))
        sel = jnp.where(jnp.logical_and(take, admissible(t)), 1.0, 0.0)
        sel_scr[t] = sel.T
        return run + jnp.sum(eq, axis=0, keepdims=True)

    lax.fori_loop(0, nt, select_tile, jnp.zeros((1, qb), F32))

    scale = ATT_HEAD_DIM ** -0.5 * LOG2_E
    hd = ATT_HEAD_DIM
    rows = ATT_GROUP * qb
    pair = LANES // hd
    for g in range(ATT_KV_HEADS):
        qg = jnp.concatenate(
            [q_ref[:, (g * ATT_GROUP + r) * hd:(g * ATT_GROUP + r + 1) * hd] for r in range(ATT_GROUP)],
            axis=0) * scale
        zeros = jnp.zeros_like(qg)
        parts = [qg if c == g % pair else zeros for c in range(pair)]
        q_scr[g] = jnp.concatenate(parts, axis=1).astype(BF16)
    m_scr[...] = jnp.full(m_scr.shape, NEG_INF, F32)
    l_scr[...] = jnp.zeros(l_scr.shape, F32)
    acc_scr[...] = jnp.zeros(acc_scr.shape, F32)

    def kv_tile(ref, t, g):
        lane0 = (g // pair) * LANES
        return ref[pl.ds(pl.multiple_of(t * kt, LANES), kt), lane0:lane0 + LANES].astype(BF16)

    def logits_tile(t, carry):
        sel = sel_scr[t][None] > 0.0
        for g in range(ATT_KV_HEADS):
            lg = _dot_nt(q_scr[g], kv_tile(k_ref, t, g)).reshape(ATT_GROUP, qb, kt)
            lg = jnp.where(sel, lg, NEG_INF).reshape(rows, kt)
            lg_scr[t, g] = lg
            m_scr[g] = jnp.maximum(m_scr[g], _tree(jnp.maximum, _lane_tiles(lg)))
        return carry

    lax.fori_loop(0, nt, logits_tile, 0)
    for g in range(ATT_KV_HEADS):
        m_scr[g] = jnp.broadcast_to(jnp.max(m_scr[g], axis=1, keepdims=True), (rows, LANES))

    def pv_tile(t, carry):
        for g in range(ATT_KV_HEADS):
            m = m_scr[g]
            ps = [jnp.exp2(lg_c - m) for lg_c in _lane_tiles(lg_scr[t, g])]
            l_scr[g] += _tree(jnp.add, ps)
            acc_scr[g] += _dot(jnp.concatenate(ps, axis=1).astype(BF16), kv_tile(v_ref, t, g))
        return carry

    lax.fori_loop(0, nt, pv_tile, 0)
    for g in range(ATT_KV_HEADS):
        denom = jnp.sum(l_scr[g], axis=1, keepdims=True)
        out = acc_scr[g][:, (g % pair) * hd:(g % pair + 1) * hd] / denom
        for r in range(ATT_GROUP):
            h = g * ATT_GROUP + r
            o_ref[:, h * hd:(h + 1) * hd] = out[r * qb:(r + 1) * qb]


def _dsa(q, qi, wi, k, v, ki, *, kt, n_valid, q_pos0, k_sel):
    b, lq, _ = q.shape
    nk = k.shape[1]
    nkt = nk // kt
    qb = LANES
    qspec = lambda w: pl.BlockSpec((None, qb, w), lambda bi, j: (bi, j, 0))
    kspec = lambda w: pl.BlockSpec((None, nk, w), lambda bi, j: (bi, 0, 0))
    rows = ATT_GROUP * qb
    assert kt >= k_sel, "a query block must see at least k_sel key slots"
    earlier = jnp.tril(jnp.ones((kt, kt), BF16), -1)
    kern = functools.partial(_dsa_kernel, kt=kt, nkt=nkt, n_valid=n_valid, q_pos0=q_pos0, k_sel=k_sel)
    return pl.pallas_call(
        kern,
        grid=(b, lq // qb),
        in_specs=[qspec(q.shape[2]), qspec(qi.shape[2]), qspec(wi.shape[2]),
                  kspec(k.shape[2]), kspec(v.shape[2]), kspec(ki.shape[2]),
                  pl.BlockSpec((kt, kt), lambda bi, j: (0, 0))],
        out_specs=qspec(q.shape[2]),
        out_shape=jax.ShapeDtypeStruct(q.shape, F32),
        scratch_shapes=[
            pltpu.VMEM((nkt, kt, qb), F32),
            pltpu.VMEM((nkt, qb, kt), F32),
            pltpu.VMEM((nkt, ATT_KV_HEADS, rows, kt), F32),
            pltpu.VMEM((ATT_KV_HEADS, rows, LANES), BF16),
            pltpu.VMEM((ATT_KV_HEADS, rows, LANES), F32),
            pltpu.VMEM((ATT_KV_HEADS, rows, LANES), F32),
            pltpu.VMEM((ATT_KV_HEADS, rows, LANES), F32),
        ],
        compiler_params=_cparams("parallel", "arbitrary"),
        name="dsa",
    )(q, qi, wi, k, v, ki, earlier)


def _silu(x):
    return x / (1.0 + jnp.exp(-x))


def _ssd_kernel(xbc_ref, z_ref, dt_ref, cprev_ref, sprev_ref, cw_ref, cb_ref, dtb_ref, alog_ref,
                dsk_ref, nw_ref, y_ref, cnew_ref, s_ref, buf_ref, yh_ref, *, l_valid):
    c = pl.program_id(1)
    lc = SSD_CHUNK
    hist = SUBLANES
    d_inner = y_ref.shape[1]
    n_heads = d_inner // SSM_HEAD_DIM
    gn = SSM_GROUPS * SSM_D_STATE
    heads_per_group = n_heads // SSM_GROUPS
    p = SSM_HEAD_DIM

    @pl.when(c == 0)
    def _():
        buf_ref[0:hist, :] = cprev_ref[...]
        s_ref[...] = sprev_ref[...]

    buf_ref[hist:hist + lc, :] = xbc_ref[...]
    conv = cb_ref[...] + sum(
        buf_ref[hist - (CONV_WIDTH - 1) + t:hist - (CONV_WIDTH - 1) + t + lc, :] * cw_ref[t:t + 1, :]
        for t in range(CONV_WIDTH))
    xbc = _silu(conv)
    cnew_ref[...] = buf_ref[l_valid:l_valid + hist, :]
    buf_ref[0:hist, :] = buf_ref[lc:lc + hist, :]

    xs = xbc[:, :d_inner]
    b_all = xbc[:, d_inner:d_inner + gn].astype(BF16)
    c_all = xbc[:, d_inner + gn:].astype(BF16)

    row = lax.broadcasted_iota(jnp.int32, (lc, LANES), 0)
    pre = dt_ref[...] + dtb_ref[...]
    dt = jnp.maximum(pre, 0.0) + jnp.log1p(jnp.exp(-jnp.abs(pre)))
    dt = jnp.where(row < l_valid, dt, 0.0)
    a = dt * (-jnp.exp(alog_ref[...]))
    ta = lax.broadcasted_iota(jnp.int32, (lc, lc), 0)
    tb = lax.broadcasted_iota(jnp.int32, (lc, lc), 1)
    tril = ta >= tb
    acum = jnp.dot(jnp.where(tril, 1.0, 0.0), a, preferred_element_type=F32,
                   precision=lax.Precision.HIGHEST)
    acum_t = acum.T
    dt_t = dt.T
    xs_t = xs.T

    for g in range(SSM_GROUPS):
        b_g = b_all[:, g * SSM_D_STATE:(g + 1) * SSM_D_STATE]
        c_g = c_all[:, g * SSM_D_STATE:(g + 1) * SSM_D_STATE]
        cb = _dot_nt(c_g, b_g)
        for hh in range(heads_per_group):
            h = g * heads_per_group + hh
            col = acum[:, h:h + 1]
            arow = acum_t[h:h + 1, :]
            dtrow = dt_t[h:h + 1, :]
            alast = acum_t[h:h + 1, lc - 1:lc]
            decay = jnp.where(tril, jnp.exp(jnp.where(tril, col - arow, 0.0)), 0.0)
            mh = (cb * decay * dtrow).astype(BF16)
            y_diag = _dot(mh, xs[:, h * p:(h + 1) * p].astype(BF16))
            s_h = s_ref[h * p:(h + 1) * p, :]
            y_off = _dot_nt(c_g, s_h.astype(BF16)) * jnp.exp(col)
            yh_ref[:, h * p:(h + 1) * p] = y_diag + y_off
            w_row = dtrow * jnp.exp(alast - arow)
            x_t = (xs_t[h * p:(h + 1) * p, :] * w_row).astype(BF16)
            s_ref[h * p:(h + 1) * p, :] = s_h * jnp.exp(alast) + _dot(x_t, b_g)

    y = yh_ref[...] + dsk_ref[...] * xs
    y = y * _silu(z_ref[...])
    gw = d_inner // SSM_GROUPS
    for g in range(SSM_GROUPS):
        yg = y[:, g * gw:(g + 1) * gw]
        ms = jnp.mean(yg * yg, axis=1, keepdims=True)
        y_ref[:, g * gw:(g + 1) * gw] = yg * lax.rsqrt(ms + RMS_EPS) * nw_ref[:, g * gw:(g + 1) * gw]


def _ssd(xbc, z_src, dt, conv_prev, ssm_prev, conv_w, conv_b, dt_bias, a_log, d_skip, norm_w, *, l_valid):
    b, l, conv_dim = xbc.shape
    d_inner = norm_w.shape[1]
    nc = l // SSD_CHUNK
    tok = lambda w: pl.BlockSpec((None, SSD_CHUNK, w), lambda bi, c: (bi, c, 0))
    per_b = lambda s: pl.BlockSpec((None,) + s, lambda bi, c: (bi, 0, 0))
    par = lambda a: pl.BlockSpec(a.shape, lambda bi, c: (0, 0))
    kern = functools.partial(_ssd_kernel, l_valid=l_valid)
    return pl.pallas_call(
        kern,
        grid=(b, nc),
        in_specs=[tok(conv_dim), tok(d_inner), tok(LANES),
                  per_b((SUBLANES, conv_dim)), per_b(ssm_prev.shape[1:]),
                  par(conv_w), par(conv_b), par(dt_bias), par(a_log), par(d_skip), par(norm_w)],
        out_specs=[tok(d_inner), per_b((SUBLANES, conv_dim)), per_b(ssm_prev.shape[1:])],
        out_shape=[jax.ShapeDtypeStruct((b, l, d_inner), F32),
                   jax.ShapeDtypeStruct((b, SUBLANES, conv_dim), F32),
                   jax.ShapeDtypeStruct(ssm_prev.shape, F32)],
        scratch_shapes=[pltpu.VMEM((SSD_CHUNK + 2 * SUBLANES, conv_dim), F32),
                        pltpu.VMEM((SSD_CHUNK, d_inner), F32)],
        compiler_params=_cparams("parallel", "arbitrary"),
        name="ssd",
    )(xbc, z_src, dt, conv_prev, ssm_prev, conv_w, conv_b, dt_bias, a_log, d_skip, norm_w)


def _layer_norm(x, g, b):
    mu = jnp.mean(x, axis=-1, keepdims=True)
    xc = x - mu
    var = jnp.mean(xc * xc, axis=-1, keepdims=True)
    return xc * lax.rsqrt(var + LN_EPS) * g + b


def _mix_kernel(x_ref, attn_ref, yssm_ref, wg_ref, bg_ref, wa_ref, wm_ref, wo_ref, g_ref, b_ref, o_ref,
                *, alpha):
    d = x_ref.shape[1]
    x = x_ref[...]
    ya = _dot(attn_ref[...].astype(BF16), wa_ref[...])
    ym = _dot(yssm_ref[...].astype(BF16), wm_ref[...])
    gates = _dot(x.astype(BF16), wg_ref[...]) + bg_ref[...]
    gates = 1.0 / (1.0 + jnp.exp(-gates))
    mixed = gates[:, :d] * ya + gates[:, d:] * ym
    out = _dot(mixed.astype(BF16), wo_ref[...])
    o_ref[...] = _layer_norm(alpha * x + out, g_ref[...], b_ref[...])


def _mix(x, attn, y_ssm, w_gate, b_gate, w_a, w_m, w_o, ln_g, ln_b, *, alpha, tm):
    m, d = x.shape
    row = lambda w: pl.BlockSpec((tm, w), lambda i: (i, 0))
    par = lambda a: pl.BlockSpec(a.shape, lambda i: (0, 0))
    return pl.pallas_call(
        functools.partial(_mix_kernel, alpha=alpha),
        grid=(m // tm,),
        in_specs=[row(d), row(attn.shape[1]), row(y_ssm.shape[1]),
                  par(w_gate), par(b_gate), par(w_a), par(w_m), par(w_o), par(ln_g), par(ln_b)],
        out_specs=row(d),
        out_shape=jax.ShapeDtypeStruct((m, d), F32),
        compiler_params=_cparams("parallel"),
        name="mix",
    )(x, attn, y_ssm, w_gate, b_gate, w_a, w_m, w_o, ln_g, ln_b)


def _top_values(x, n, with_rank=False):
    vals = []
    rank = jnp.full(x.shape, float(n), F32)
    for i in range(n):
        m = jnp.max(x, axis=0, keepdims=True)
        vals.append(m)
        hit = x == m
        if with_rank:
            rank = jnp.where(hit, float(i), rank)
        x = jnp.where(hit, -jnp.inf, x)
    return (vals, rank) if with_rank else vals


def _gelu(x):
    return 0.5 * x * (1.0 + lax.erf(x * (2.0 ** -0.5)))


def _peer_kernel(h_ref, wqt_ref, k1_ref, k2_ref, u_ref, vt_ref, g_ref, b_ref, o_ref,
                 ht_scr, nsel_scr, rank_scr, e1_scr, e2_scr, act_scr, wa_scr, acc_scr, *, alpha, i1_per_step):
    s = pl.program_id(1)
    n_blocks = pl.num_programs(1) - 1
    half = PEER_QDIM // 2
    nk = PEER_NKEYS

    @pl.when(s == 0)
    def _():
        ht = h_ref[...].T.astype(BF16)
        ht_scr[...] = ht
        k1 = k1_ref[...].astype(BF16)
        k2 = k2_ref[...].astype(BF16)
        for hd in range(PEER_HEADS):
            qt = _dot(wqt_ref[hd * PEER_QDIM:(hd + 1) * PEER_QDIM, :], ht).astype(BF16)
            s1 = _dot(k1, qt[:half])
            s2 = _dot(k2, qt[half:])
            v1 = _top_values(s1, PEER_TOPK)
            v2, rank2 = _top_values(s2, PEER_TOPK, with_rank=True)
            cand = jnp.concatenate(
                [v1[i] + v2[jj] for i in range(PEER_TOPK) for jj in range(PEER_TOPK // (i + 1))], axis=0)
            top = _top_values(cand, PEER_TOPK)
            tau = top[PEER_TOPK - 1]
            zsum = sum(jnp.exp(t - top[0]) for t in top)
            n_sel = jnp.zeros(s1.shape, F32)
            for jj in range(PEER_TOPK):
                n_sel = jnp.where(s1 + v2[jj] >= tau, float(jj + 1), n_sel)
            nsel_scr[hd] = n_sel
            rank_scr[hd] = rank2.astype(BF16)
            e1_scr[hd] = jnp.exp(s1 - v1[0])
            e2_scr[hd] = (jnp.exp(s2 - v2[0]) / zsum).astype(BF16)
        acc_scr[...] = jnp.zeros(acc_scr.shape, F32)
        act_scr[...] = jnp.zeros(act_scr.shape, F32)

    prev = jnp.maximum(s - 1, 0)
    tt = acc_scr.shape[1]
    n_cols = tt // LANES
    dk = ht_scr.shape[0] // n_cols
    rb = 4 * SUBLANES

    def u_chunk(c):
        return _dot(u_ref[:, c * dk:(c + 1) * dk], ht_scr[c * dk:(c + 1) * dk, :])

    def i1_rows(scr):
        return [[scr[hd, pl.ds(prev * i1_per_step + ii, 1), :].astype(BF16) for ii in range(i1_per_step)]
                for hd in range(PEER_HEADS)]

    nsel_rows = i1_rows(nsel_scr)
    cw_rows = i1_rows(e1_scr)

    def gate_cols(c):
        cols = slice(c * LANES, (c + 1) * LANES)
        for r0 in range(0, nk, rb):
            w = [jnp.zeros((rb, LANES), BF16) for _ in range(i1_per_step)]
            for hd in range(PEER_HEADS):
                rk = rank_scr[hd, r0:r0 + rb, cols]
                e2t = e2_scr[hd, r0:r0 + rb, cols]
                for ii in range(i1_per_step):
                    ns = nsel_rows[hd][ii][:, cols]
                    cw = cw_rows[hd][ii][:, cols]
                    w[ii] = w[ii] + jnp.where(rk < ns, e2t * cw, jnp.zeros((), BF16))
            for ii in range(i1_per_step):
                rows = slice(ii * nk + r0, ii * nk + r0 + rb)
                wa_scr[rows, cols] = w[ii] * act_scr[(s + 1) % 2, rows, cols].astype(BF16)

    def second(c0, c1):
        cols = slice(c0 * LANES, c1 * LANES)
        acc_scr[:, cols] += _dot(vt_ref[...], wa_scr[:, cols])

    pre = None
    group = 2 if n_cols % 2 == 0 else 1
    for c in range(n_cols):
        uc = u_chunk(c)
        pre = uc if pre is None else pre + uc
        gate_cols(c)
        if (c + 1) % group == 0:
            second(c + 1 - group, c + 1)
    act_scr[s % 2] = _gelu(pre)

    @pl.when(s == pl.num_programs(1) - 1)
    def _():
        hblk = h_ref[...]
        o_ref[...] = _layer_norm(alpha * hblk + acc_scr[...].T, g_ref[...], b_ref[...])


def _peer(h, wq_t, keys1, keys2, u, v_t, ln_g, ln_b, *, alpha, tt, i1_per_step):
    m, d = h.shape
    n_exp = u.shape[0]
    eb = i1_per_step * PEER_NKEYS
    n_blocks = n_exp // eb
    par = lambda a: pl.BlockSpec(a.shape, lambda i, s: (0, 0))
    scr = lambda dt: pltpu.VMEM((PEER_HEADS, PEER_NKEYS, tt), dt)
    return pl.pallas_call(
        functools.partial(_peer_kernel, alpha=alpha, i1_per_step=i1_per_step),
        grid=(m // tt, n_blocks + 1),
        in_specs=[pl.BlockSpec((tt, d), lambda i, s: (i, 0)), par(wq_t), par(keys1), par(keys2),
                  pl.BlockSpec((eb, d), lambda i, s: (jnp.minimum(s, n_blocks - 1), 0)),
                  pl.BlockSpec((d, eb), lambda i, s: (0, jnp.maximum(s - 1, 0))),
                  par(ln_g), par(ln_b)],
        out_specs=pl.BlockSpec((tt, d), lambda i, s: (i, 0)),
        out_shape=jax.ShapeDtypeStruct((m, d), F32),
        scratch_shapes=[pltpu.VMEM((d, tt), BF16), scr(F32), scr(BF16), scr(F32), scr(BF16),
                        pltpu.VMEM((2, eb, tt), F32), pltpu.VMEM((eb, tt), BF16), pltpu.VMEM((d, tt), F32)],
        compiler_params=_cparams("parallel", "arbitrary"),
        name="peer",
    )(h, wq_t, keys1, keys2, u, v_t, ln_g, ln_b)


def _transpose_cast_kernel(x_ref, o_ref):
    o_ref[...] = x_ref[...].T.astype(o_ref.dtype)


def _transpose_cast(x, dtype, tr):
    r, c = x.shape
    return pl.pallas_call(
        _transpose_cast_kernel,
        grid=(r // tr,),
        in_specs=[pl.BlockSpec((tr, c), lambda i: (i, 0))],
        out_specs=pl.BlockSpec((c, tr), lambda i: (0, i)),
        out_shape=jax.ShapeDtypeStruct((c, r), dtype),
        compiler_params=_cparams("parallel"),
        name="transpose_cast",
    )(x)


def _pad_cols(w, width):
    return jnp.pad(w, ((0, 0), (0, width - w.shape[1])))


def _split_w_in_kernel(w_ref, a_ref, xbc_ref, z_ref, g_ref, *, offs):
    lead = offs[5]
    wi_at = -(-lead // LANES) * LANES
    dt_at = wi_at + LANES
    a_ref[...] = jnp.zeros(a_ref.shape, BF16)
    a_ref[:, 0:lead] = w_ref[:, 0:lead].astype(BF16)
    a_ref[:, wi_at:wi_at + offs[6] - offs[5]] = w_ref[:, offs[5]:offs[6]].astype(BF16)
    a_ref[:, dt_at:dt_at + offs[9] - offs[8]] = w_ref[:, offs[8]:offs[9]].astype(BF16)
    z_ref[...] = w_ref[:, offs[6]:offs[7]].astype(BF16)
    xbc_ref[...] = w_ref[:, offs[7]:offs[8]].astype(BF16)
    g_ref[...] = w_ref[:, offs[9]:offs[10]].astype(BF16)


def _split_w_in(w_in, d_model, d_inner, conv_dim, n_ssm_heads):
    att_w = ATT_HEADS * ATT_HEAD_DIM
    kv_w = ATT_KV_HEADS * ATT_HEAD_DIM
    splits = (att_w, kv_w, kv_w, IDX_HEADS * IDX_DIM, IDX_DIM, IDX_HEADS,
              d_inner, conv_dim, n_ssm_heads, N_BRANCHES * d_model)
    offs = tuple(int(o) for o in np.cumsum((0,) + splits))
    d, n = w_in.shape
    a_w = -(-offs[5] // LANES) * LANES + 2 * LANES
    widths = (a_w, conv_dim, d_inner, N_BRANCHES * d_model)
    tr = LANES
    return pl.pallas_call(
        functools.partial(_split_w_in_kernel, offs=offs),
        grid=(d // tr,),
        in_specs=[pl.BlockSpec((tr, n), lambda i: (i, 0))],
        out_specs=[pl.BlockSpec((tr, w), lambda i: (i, 0)) for w in widths],
        out_shape=[jax.ShapeDtypeStruct((d, w), BF16) for w in widths],
        compiler_params=_cparams("parallel"),
        name="split_w_in",
    )(w_in)


def _tiles(m, n_keys):
    kt = min((512, 384, 256), key=lambda t: (-(-n_keys // t) * t, -t))
    return min(m, 512), min(m, 1024), min(m, 512), kt


def _stream(x, pos, past, wts):
    (w_a, w_xbc, w_z, w_gate, b_gate, conv_w, conv_b, dt_bias, a_log, d_skip_row, norm_w, w_att, w_ssm, w_out,
     ln1_g, ln1_b, wq_t, keys1, keys2, u_exp, v_exp, ln2_g, ln2_b, alpha) = wts
    b, l, d = x.shape
    m = b * l
    xf = x.reshape(m, d)
    conv_dim = w_xbc.shape[1]
    d_inner = norm_w.shape[1]
    tm, tm_mm, tt, kt = _tiles(m, l if past is None else past[0].shape[1] + l)

    tables = _rope_tables(jnp.asarray(np.tile(pos, max(1, tm // l))))
    q, k, v, qi, ki, wi, dt = _proj_a(xf, w_a, tables, tm)
    xbc = _matmul(xf, w_xbc, tm_mm, 1024)
    z = _matmul(xf, w_z, tm_mm, 1024)

    r3 = lambda a: a.reshape(b, l, a.shape[1])
    if past is None:
        k_all, v_all, ki_all = r3(k), r3(v), r3(ki)
        n_valid = l
        k_sel = min(TOPK_MAX, l // 4)
        conv_prev = jnp.zeros((b, SUBLANES, conv_dim), F32)
        ssm_prev = jnp.zeros((b, d_inner, SSM_D_STATE), F32)
    else:
        ck, cv, cki, conv_state, ssm_state = past
        n_past = ck.shape[1]
        n_valid = n_past + l
        n_pad = -n_valid % kt
        cat = lambda c, new: jnp.pad(jnp.concatenate([c.reshape(b, n_past, -1), r3(new)], axis=1),
                                     ((0, 0), (0, n_pad), (0, 0)))
        k_all, v_all, ki_all = cat(ck, k), cat(cv, v), cat(cki, ki)
        k_sel = min(TOPK_MAX, n_valid // 4)
        conv_prev = jnp.pad(conv_state, ((0, 0), (SUBLANES - (CONV_WIDTH - 1), 0), (0, 0)))
        ssm_prev = ssm_state.reshape(b, d_inner, SSM_D_STATE)
    q_pad = -l % LANES
    padq = lambda a: jnp.pad(r3(a), ((0, 0), (0, q_pad), (0, 0)))
    attn = _dsa(padq(q), padq(qi), padq(wi), k_all, v_all, ki_all,
                kt=kt, n_valid=n_valid, q_pos0=int(pos[0]), k_sel=k_sel)[:, :l]

    l_pad = -l % SSD_CHUNK
    padl = lambda a: jnp.pad(r3(a), ((0, 0), (0, l_pad), (0, 0)))
    y_ssm, conv_new, ssm_new = _ssd(padl(xbc), padl(z), padl(dt), conv_prev, ssm_prev,
                                    conv_w, conv_b, dt_bias, a_log, d_skip_row, norm_w,
                                    l_valid=SSD_CHUNK if l_pad == 0 else l)
    y_ssm = y_ssm[:, :l].reshape(m, d_inner)
    conv_new = conv_new[:, SUBLANES - (CONV_WIDTH - 1):]
    n_ssm_heads = d_inner // SSM_HEAD_DIM
    ssm_new = ssm_new.reshape(b, n_ssm_heads, SSM_HEAD_DIM, SSM_D_STATE)

    h1 = _mix(xf, attn.reshape(m, -1), y_ssm, w_gate, b_gate, w_att, w_ssm, w_out, ln1_g, ln1_b,
              alpha=alpha, tm=tm)
    y = _peer(h1, wq_t, keys1, keys2, u_exp, v_exp, ln2_g, ln2_b, alpha=alpha, tt=tt, i1_per_step=4)
    state = (r3(k).reshape(b, l, ATT_KV_HEADS, ATT_HEAD_DIM), r3(v).reshape(b, l, ATT_KV_HEADS, ATT_HEAD_DIM),
             r3(ki), conv_new, ssm_new)
    return y.reshape(b, l, d), state


def kernel(x_prompt, x_sample, cache_k, cache_v, cache_kidx, state_conv, state_ssm, w_in, b_gate, conv_w, conv_b, dt_bias, a_log, d_skip, ssm_norm_w, w_attn_br, w_ssm_br, w_out, ln1_g, ln1_b, peer_wq, peer_keys1, peer_keys2, peer_u, peer_v, ln2_g, ln2_b):
    depth = w_in.shape[0]
    d_model = x_prompt.shape[2]
    d_inner = ssm_norm_w.shape[1]
    conv_dim = conv_w.shape[2]
    n_ssm_heads = a_log.shape[1]
    alpha = (2.0 * depth) ** 0.25
    pos_p = np.arange(x_prompt.shape[1])
    pos_s = cache_k.shape[2] + np.arange(x_sample.shape[1])

    hp, hs = x_prompt, x_sample
    new_p, new_s = [], []
    for l in range(depth):
        w_a, w_xbc, w_z, w_gate = _split_w_in(w_in[l], d_model, d_inner, conv_dim, n_ssm_heads)
        row = lambda a: a.reshape(1, -1)
        wts = (w_a, w_xbc, w_z, w_gate, row(b_gate[l]),
               jnp.pad(conv_w[l], ((0, SUBLANES - CONV_WIDTH), (0, 0))), row(conv_b[l]),
               _pad_cols(row(dt_bias[l]), LANES), _pad_cols(row(a_log[l]), LANES),
               row(jnp.repeat(d_skip[l], SSM_HEAD_DIM)), row(ssm_norm_w[l]),
               w_attn_br[l].astype(BF16), w_ssm_br[l].astype(BF16), w_out[l].astype(BF16),
               row(ln1_g[l]), row(ln1_b[l]),
               _transpose_cast(peer_wq[l], BF16, 512), peer_keys1[l], peer_keys2[l],
               peer_u[l].astype(BF16), _transpose_cast(peer_v[l], BF16, 512), row(ln2_g[l]), row(ln2_b[l]), alpha)
        hp, sp = _stream(hp, pos_p, None, wts)
        past = (cache_k[l], cache_v[l], cache_kidx[l], state_conv[l], state_ssm[l])
        hs, ss = _stream(hs, pos_s, past, wts)
        new_p.append(sp)
        new_s.append(ss)

    stack = lambda lst, i: jnp.stack([e[i] for e in lst], axis=0)
    return (hp, hs) + tuple(stack(new_p, i) for i in range(5)) + tuple(stack(new_s, i) for i in range(5))
```

```python
import functools

import jax
import jax.numpy as jnp
import numpy as np
from jax import lax
from jax.experimental import pallas as pl
from jax.experimental.pallas import tpu as pltpu

F32 = jnp.float32
BF16 = jnp.bfloat16

LANES = 128
SUBLANES = 8
VMEM_LIMIT_BYTES = 48 * 1024 * 1024

CHUNK = 64
ATT_HEADS = 16
ATT_HEAD_DIM = 64
ATT_KV_HEADS = 4
ATT_GROUP = ATT_HEADS // ATT_KV_HEADS
IDX_HEADS = 4
IDX_DIM = 64
TOPK_MAX = 256
ROPE_THETA = 500000.0
ROPE_FRACTION = 4
SSM_HEAD_DIM = 64
SSM_GROUPS = 4
SSM_D_STATE = 128
CONV_WIDTH = 4
N_BRANCHES = 2
PEER_HEADS = 8
PEER_NKEYS = 128
PEER_QDIM = 256
PEER_TOPK = 16
LN_EPS = 1e-5
RMS_EPS = 1e-5
NEG_INF = -1e30
SSD_CHUNK = 128
LOG2_E = 1.4426950408889634
INT32_MIN = -(2 ** 31)
CODE_NEG_INFINITY = INT32_MIN + 0x7FFFFF


def _cparams(*sem):
    return pltpu.CompilerParams(dimension_semantics=sem, vmem_limit_bytes=VMEM_LIMIT_BYTES)


def _dot(a, b):
    return jnp.dot(a, b, preferred_element_type=F32)


def _dot_nt(a, b):
    return lax.dot_general(a, b, (((1,), (1,)), ((), ())), preferred_element_type=F32)


def _rope_tables(pos):
    rd = ATT_HEAD_DIM // ROPE_FRACTION
    half = rd // 2
    inv = ROPE_THETA ** (-(jnp.arange(half, dtype=F32) * 2.0) / rd)
    ang = pos.astype(F32)[:, None] * inv[None, :]
    cos, sin = jnp.cos(ang), jnp.sin(ang)
    n = pos.shape[0]
    ones = jnp.ones((n, ATT_HEAD_DIM - rd), F32)
    zeros = jnp.zeros((n, ATT_HEAD_DIM - rd), F32)
    zh = jnp.zeros((n, half), F32)
    cos_h = jnp.concatenate([cos, cos, ones], axis=1)
    sa_h = jnp.concatenate([-sin, zh, zeros], axis=1)
    sb_h = jnp.concatenate([zh, sin, zeros], axis=1)
    rep = LANES // ATT_HEAD_DIM
    return jnp.tile(cos_h, (1, rep)), jnp.tile(sa_h, (1, rep)), jnp.tile(sb_h, (1, rep))


def _proj_a_kernel(x_ref, w_ref, cos_ref, sa_ref, sb_ref,
                   q_ref, k_ref, v_ref, qi_ref, ki_ref, wi_ref, dt_ref):
    x = x_ref[...].astype(BF16)
    cos, sa, sb = cos_ref[...], sa_ref[...], sb_ref[...]
    half = ATT_HEAD_DIM // ROPE_FRACTION // 2

    def rope(t):
        up = pltpu.roll(t, LANES - half, 1)
        down = pltpu.roll(t, half, 1)
        return t * cos + up * sa + down * sb

    def tiles(c0, n):
        out = []
        for c in range(c0, c0 + n, 2):
            w = min(2, c0 + n - c)
            out += _lane_tiles(_dot(x, w_ref[:, c * LANES:(c + w) * LANES]))
        return out

    nq = q_ref.shape[1] // LANES
    nk = k_ref.shape[1] // LANES
    nqi = qi_ref.shape[1] // LANES
    c = 0
    for ref, n, rotary in ((q_ref, nq, True), (k_ref, nk, True), (v_ref, nk, False), (qi_ref, nqi, True)):
        for j, t in enumerate(tiles(c, n)):
            ref[:, j * LANES:(j + 1) * LANES] = rope(t) if rotary else t
        c += n
    ki, wi, dt = tiles(c, 3)
    ki_ref[...] = rope(ki)[:, :IDX_DIM]
    wi_ref[...] = wi
    dt_ref[...] = dt


def _proj_a(x, w_a, tables, tm):
    m, d = x.shape
    att_w = ATT_HEADS * ATT_HEAD_DIM
    kv_w = ATT_KV_HEADS * ATT_HEAD_DIM
    idx_w = IDX_HEADS * IDX_DIM
    tab_blocks = tables[0].shape[0] // tm
    row = lambda i: (i, 0)
    tab = lambda i: (i % tab_blocks, 0)
    full = lambda i: (0, 0)
    out_w = (att_w, kv_w, kv_w, idx_w, IDX_DIM, LANES, LANES)
    return pl.pallas_call(
        _proj_a_kernel,
        grid=(m // tm,),
        in_specs=[pl.BlockSpec((tm, d), row), pl.BlockSpec(w_a.shape, full)]
                 + [pl.BlockSpec((tm, LANES), tab)] * 3,
        out_specs=[pl.BlockSpec((tm, w), row) for w in out_w],
        out_shape=[jax.ShapeDtypeStruct((m, w), F32) for w in out_w],
        compiler_params=_cparams("parallel"),
        name="proj_a",
    )(x, w_a, *tables)


def _matmul_kernel(x_ref, w_ref, o_ref):
    o_ref[...] = _dot(x_ref[...].astype(BF16), w_ref[...])


def _matmul(x, w, tm, tn):
    m, k = x.shape
    n = w.shape[1]
    return pl.pallas_call(
        _matmul_kernel,
        grid=(m // tm, n // tn),
        in_specs=[pl.BlockSpec((tm, k), lambda i, j: (i, 0)), pl.BlockSpec((k, tn), lambda i, j: (0, j))],
        out_specs=pl.BlockSpec((tm, tn), lambda i, j: (i, j)),
        out_shape=jax.ShapeDtypeStruct((m, n), F32),
        compiler_params=_cparams("parallel", "arbitrary"),
        name="matmul",
    )(x, w)


def _tree(op, xs):
    xs = list(xs)
    while len(xs) > 1:
        xs = [op(xs[i], xs[i + 1]) for i in range(0, len(xs) - 1, 2)] + ([xs[-1]] if len(xs) % 2 else [])
    return xs[0]


def _lane_tiles(x):
    return [x[:, c * LANES:(c + 1) * LANES] for c in range(x.shape[1] // LANES)]


def _dsa_kernel(q_ref, qi_ref, wi_ref, k_ref, v_ref, ki_ref, earlier_ref, o_ref,
                score_scr, sel_scr, lg_scr, q_scr, m_scr, l_scr, acc_scr,
                *, kt, nkt, n_valid, q_pos0, k_sel):
    qb = LANES
    j = pl.program_id(1)
    q_first = q_pos0 + j * qb
    nt = jnp.minimum(nkt, (q_first + qb + kt - 1) // kt)

    key_off = lax.broadcasted_iota(jnp.int32, (kt, qb), 0)
    q_chunk = (q_first + lax.broadcasted_iota(jnp.int32, (1, qb), 1)) // CHUNK
    key_limit = jnp.minimum((q_chunk + 1) * CHUNK, n_valid)

    def admissible(t):
        return key_off < key_limit - t * kt

    qi_t = qi_ref[...].T.astype(BF16)
    wi_t = wi_ref[...].T

    def score_tile(t, carry):
        ki_t = ki_ref[pl.ds(pl.multiple_of(t * kt, LANES), kt), :].astype(BF16)
        s = jnp.zeros((kt, qb), F32)
        for h in range(IDX_HEADS):
            lg = _dot(ki_t, qi_t[h * IDX_DIM:(h + 1) * IDX_DIM, :])
            s = s + jnp.maximum(lg, 0.0) * wi_t[h:h + 1, :]
        s = jnp.where(s == 0.0, 0.0, s)
        score_scr[t] = jnp.where(admissible(t), s, NEG_INF)
        return carry

    lax.fori_loop(0, nt, score_tile, 0)

    def count(pred):
        def body(t, acc):
            hit = jnp.where(pred(score_scr[t]), 1.0, 0.0)
            return acc + _tree(jnp.add, [hit[r:r + SUBLANES] for r in range(0, kt, SUBLANES)])
        acc = lax.fori_loop(0, nt, body, jnp.zeros((SUBLANES, qb), F32))
        return jnp.sum(acc, axis=0, keepdims=True)

    kf = jnp.float32(k_sel)

    def decode(code):
        bits = jnp.where(code < 0, code ^ jnp.int32(0x7FFFFFFF), code)
        return lax.bitcast_convert_type(bits, F32)

    def at_least_k(code):
        thr_f = decode(code)
        return jnp.logical_or(count(lambda sc: sc >= thr_f) >= kf, code < CODE_NEG_INFINITY)

    zero = jnp.zeros((1, qb), jnp.int32)
    code = jnp.where(count(lambda sc: sc >= 0.0) >= kf, zero, jnp.int32(INT32_MIN))

    def bit_pass(i, code):
        cand = code | jnp.left_shift(jnp.int32(1), 30 - i)
        return jnp.where(at_least_k(cand), cand, code)

    thr = decode(lax.fori_loop(0, 31, bit_pass, code))

    need = kf - count(lambda sc: sc > thr)
    earlier = earlier_ref[...]

    def select_tile(t, run):
        sc = score_scr[t]
        tie = sc == thr
        eq = jnp.where(tie, 1.0, 0.0)
        rank = _dot(earlier, eq.astype(BF16)) + run
        take = jnp.logical_or(sc > thr, jnp.logical_and(tie, rank < need))
        sel = jnp.where(jnp.logical_and(take, admissible(t)), 1.0, 0.0)
        sel_scr[t] = sel.T
        return run + jnp.sum(eq, axis=0, keepdims=True)

    lax.fori_loop(0, nt, select_tile, jnp.zeros((1, qb), F32))

    scale = ATT_HEAD_DIM ** -0.5 * LOG2_E
    hd = ATT_HEAD_DIM
    rows = ATT_GROUP * qb
    pair = LANES // hd
    for g in range(ATT_KV_HEADS):
        qg = jnp.concatenate(
            [q_ref[:, (g * ATT_GROUP + r) * hd:(g * ATT_GROUP + r + 1) * hd] for r in range(ATT_GROUP)],
            axis=0) * scale
        zeros = jnp.zeros_like(qg)
        parts = [qg if c == g % pair else zeros for c in range(pair)]
        q_scr[g] = jnp.concatenate(parts, axis=1).astype(BF16)
    m_scr[...] = jnp.full(m_scr.shape, NEG_INF, F32)
    l_scr[...] = jnp.zeros(l_scr.shape, F32)
    acc_scr[...] = jnp.zeros(acc_scr.shape, F32)

    def kv_tile(ref, t, g):
        lane0 = (g // pair) * LANES
        return ref[pl.ds(pl.multiple_of(t * kt, LANES), kt), lane0:lane0 + LANES].astype(BF16)

    def logits_tile(t, carry):
        sel = sel_scr[t][None] > 0.0
        for g in range(ATT_KV_HEADS):
            lg = _dot_nt(q_scr[g], kv_tile(k_ref, t, g)).reshape(ATT_GROUP, qb, kt)
            lg = jnp.where(sel, lg, NEG_INF).reshape(rows, kt)
            lg_scr[t, g] = lg
            m_scr[g] = jnp.maximum(m_scr[g], _tree(jnp.maximum, _lane_tiles(lg)))
        return carry

    lax.fori_loop(0, nt, logits_tile, 0)
    for g in range(ATT_KV_HEADS):
        m_scr[g] = jnp.broadcast_to(jnp.max(m_scr[g], axis=1, keepdims=True), (rows, LANES))

    def pv_tile(t, carry):
        for g in range(ATT_KV_HEADS):
            m = m_scr[g]
            ps = [jnp.exp2(lg_c - m) for lg_c in _lane_tiles(lg_scr[t, g])]
            l_scr[g] += _tree(jnp.add, ps)
            acc_scr[g] += _dot(jnp.concatenate(ps, axis=1).astype(BF16), kv_tile(v_ref, t, g))
        return carry

    lax.fori_loop(0, nt, pv_tile, 0)
    for g in range(ATT_KV_HEADS):
        denom = jnp.sum(l_scr[g], axis=1, keepdims=True)
        out = acc_scr[g][:, (g % pair) * hd:(g % pair + 1) * hd] / denom
        for r in range(ATT_GROUP):
            h = g * ATT_GROUP + r
            o_ref[:, h * hd:(h + 1) * hd] = out[r * qb:(r + 1) * qb]


def _dsa(q, qi, wi, k, v, ki, *, kt, n_valid, q_pos0, k_sel):
    b, lq, _ = q.shape
    nk = k.shape[1]
    nkt = nk // kt
    qb = LANES
    qspec = lambda w: pl.BlockSpec((None, qb, w), lambda bi, j: (bi, j, 0))
    kspec = lambda w: pl.BlockSpec((None, nk, w), lambda bi, j: (bi, 0, 0))
    rows = ATT_GROUP * qb
    assert kt >= k_sel, "a query block must see at least k_sel key slots"
    earlier = jnp.tril(jnp.ones((kt, kt), BF16), -1)
    kern = functools.partial(_dsa_kernel, kt=kt, nkt=nkt, n_valid=n_valid, q_pos0=q_pos0, k_sel=k_sel)
    return pl.pallas_call(
        kern,
        grid=(b, lq // qb),
        in_specs=[qspec(q.shape[2]), qspec(qi.shape[2]), qspec(wi.shape[2]),
                  kspec(k.shape[2]), kspec(v.shape[2]), kspec(ki.shape[2]),
                  pl.BlockSpec((kt, kt), lambda bi, j: (0, 0))],
        out_specs=qspec(q.shape[2]),
        out_shape=jax.ShapeDtypeStruct(q.shape, F32),
        scratch_shapes=[
            pltpu.VMEM((nkt, kt, qb), F32),
            pltpu.VMEM((nkt, qb, kt), F32),
            pltpu.VMEM((nkt, ATT_KV_HEADS, rows, kt), F32),
            pltpu.VMEM((ATT_KV_HEADS, rows, LANES), BF16),
            pltpu.VMEM((ATT_KV_HEADS, rows, LANES), F32),
            pltpu.VMEM((ATT_KV_HEADS, rows, LANES), F32),
            pltpu.VMEM((ATT_KV_HEADS, rows, LANES), F32),
        ],
        compiler_params=_cparams("parallel", "arbitrary"),
        name="dsa",
    )(q, qi, wi, k, v, ki, earlier)


def _silu(x):
    return x / (1.0 + jnp.exp(-x))


def _ssd_kernel(xbc_ref, z_ref, dt_ref, cprev_ref, sprev_ref, cw_ref, cb_ref, dtb_ref, alog_ref,
                dsk_ref, nw_ref, y_ref, cnew_ref, s_ref, buf_ref, yh_ref, *, l_valid):
    c = pl.program_id(1)
    lc = SSD_CHUNK
    hist = SUBLANES
    d_inner = y_ref.shape[1]
    n_heads = d_inner // SSM_HEAD_DIM
    gn = SSM_GROUPS * SSM_D_STATE
    heads_per_group = n_heads // SSM_GROUPS
    p = SSM_HEAD_DIM

    @pl.when(c == 0)
    def _():
        buf_ref[0:hist, :] = cprev_ref[...]
        s_ref[...] = sprev_ref[...]

    buf_ref[hist:hist + lc, :] = xbc_ref[...]
    conv = cb_ref[...] + sum(
        buf_ref[hist - (CONV_WIDTH - 1) + t:hist - (CONV_WIDTH - 1) + t + lc, :] * cw_ref[t:t + 1, :]
        for t in range(CONV_WIDTH))
    xbc = _silu(conv)
    cnew_ref[...] = buf_ref[l_valid:l_valid + hist, :]
    buf_ref[0:hist, :] = buf_ref[lc:lc + hist, :]

    xs = xbc[:, :d_inner]
    b_all = xbc[:, d_inner:d_inner + gn].astype(BF16)
    c_all = xbc[:, d_inner + gn:].astype(BF16)

    row = lax.broadcasted_iota(jnp.int32, (lc, LANES), 0)
    pre = dt_ref[...] + dtb_ref[...]
    dt = jnp.maximum(pre, 0.0) + jnp.log1p(jnp.exp(-jnp.abs(pre)))
    dt = jnp.where(row < l_valid, dt, 0.0)
    a = dt * (-jnp.exp(alog_ref[...]))
    ta = lax.broadcasted_iota(jnp.int32, (lc, lc), 0)
    tb = lax.broadcasted_iota(jnp.int32, (lc, lc), 1)
    tril = ta >= tb
    acum = jnp.dot(jnp.where(tril, 1.0, 0.0), a, preferred_element_type=F32,
                   precision=lax.Precision.HIGHEST)
    acum_t = acum.T
    dt_t = dt.T
    xs_t = xs.T

    for g in range(SSM_GROUPS):
        b_g = b_all[:, g * SSM_D_STATE:(g + 1) * SSM_D_STATE]
        c_g = c_all[:, g * SSM_D_STATE:(g + 1) * SSM_D_STATE]
        cb = _dot_nt(c_g, b_g)
        for hh in range(heads_per_group):
            h = g * heads_per_group + hh
            col = acum[:, h:h + 1]
            arow = acum_t[h:h + 1, :]
            dtrow = dt_t[h:h + 1, :]
            alast = acum_t[h:h + 1, lc - 1:lc]
            decay = jnp.where(tril, jnp.exp(jnp.where(tril, col - arow, 0.0)), 0.0)
            mh = (cb * decay * dtrow).astype(BF16)
            y_diag = _dot(mh, xs[:, h * p:(h + 1) * p].astype(BF16))
            s_h = s_ref[h * p:(h + 1) * p, :]
            y_off = _dot_nt(c_g, s_h.astype(BF16)) * jnp.exp(col)
            yh_ref[:, h * p:(h + 1) * p] = y_diag + y_off
            w_row = dtrow * jnp.exp(alast - arow)
            x_t = (xs_t[h * p:(h + 1) * p, :] * w_row).astype(BF16)
            s_ref[h * p:(h + 1) * p, :] = s_h * jnp.exp(alast) + _dot(x_t, b_g)

    y = yh_ref[...] + dsk_ref[...] * xs
    y = y * _silu(z_ref[...])
    gw = d_inner // SSM_GROUPS
    for g in range(SSM_GROUPS):
        yg = y[:, g * gw:(g + 1) * gw]
        ms = jnp.mean(yg * yg, axis=1, keepdims=True)
        y_ref[:, g * gw:(g + 1) * gw] = yg * lax.rsqrt(ms + RMS_EPS) * nw_ref[:, g * gw:(g + 1) * gw]


def _ssd(xbc, z_src, dt, conv_prev, ssm_prev, conv_w, conv_b, dt_bias, a_log, d_skip, norm_w, *, l_valid):
    b, l, conv_dim = xbc.shape
    d_inner = norm_w.shape[1]
    nc = l // SSD_CHUNK
    tok = lambda w: pl.BlockSpec((None, SSD_CHUNK, w), lambda bi, c: (bi, c, 0))
    per_b = lambda s: pl.BlockSpec((None,) + s, lambda bi, c: (bi, 0, 0))
    par = lambda a: pl.BlockSpec(a.shape, lambda bi, c: (0, 0))
    kern = functools.partial(_ssd_kernel, l_valid=l_valid)
    return pl.pallas_call(
        kern,
        grid=(b, nc),
        in_specs=[tok(conv_dim), tok(d_inner), tok(LANES),
                  per_b((SUBLANES, conv_dim)), per_b(ssm_prev.shape[1:]),
                  par(conv_w), par(conv_b), par(dt_bias), par(a_log), par(d_skip), par(norm_w)],
        out_specs=[tok(d_inner), per_b((SUBLANES, conv_dim)), per_b(ssm_prev.shape[1:])],
        out_shape=[jax.ShapeDtypeStruct((b, l, d_inner), F32),
                   jax.ShapeDtypeStruct((b, SUBLANES, conv_dim), F32),
                   jax.ShapeDtypeStruct(ssm_prev.shape, F32)],
        scratch_shapes=[pltpu.VMEM((SSD_CHUNK + 2 * SUBLANES, conv_dim), F32),
                        pltpu.VMEM((SSD_CHUNK, d_inner), F32)],
        compiler_params=_cparams("parallel", "arbitrary"),
        name="ssd",
    )(xbc, z_src, dt, conv_prev, ssm_prev, conv_w, conv_b, dt_bias, a_log, d_skip, norm_w)


def _layer_norm(x, g, b):
    mu = jnp.mean(x, axis=-1, keepdims=True)
    xc = x - mu
    var = jnp.mean(xc * xc, axis=-1, keepdims=True)
    return xc * lax.rsqrt(var + LN_EPS) * g + b


def _mix_kernel(x_ref, attn_ref, yssm_ref, wg_ref, bg_ref, wa_ref, wm_ref, wo_ref, g_ref, b_ref, o_ref,
                *, alpha):
    d = x_ref.shape[1]
    x = x_ref[...]
    ya = _dot(attn_ref[...].astype(BF16), wa_ref[...])
    ym = _dot(yssm_ref[...].astype(BF16), wm_ref[...])
    gates = _dot(x.astype(BF16), wg_ref[...]) + bg_ref[...]
    gates = 1.0 / (1.0 + jnp.exp(-gates))
    mixed = gates[:, :d] * ya + gates[:, d:] * ym
    out = _dot(mixed.astype(BF16), wo_ref[...])
    o_ref[...] = _layer_norm(alpha * x + out, g_ref[...], b_ref[...])


def _mix(x, attn, y_ssm, w_gate, b_gate, w_a, w_m, w_o, ln_g, ln_b, *, alpha, tm):
    m, d = x.shape
    row = lambda w: pl.BlockSpec((tm, w), lambda i: (i, 0))
    par = lambda a: pl.BlockSpec(a.shape, lambda i: (0, 0))
    return pl.pallas_call(
        functools.partial(_mix_kernel, alpha=alpha),
        grid=(m // tm,),
        in_specs=[row(d), row(attn.shape[1]), row(y_ssm.shape[1]),
                  par(w_gate), par(b_gate), par(w_a), par(w_m), par(w_o), par(ln_g), par(ln_b)],
        out_specs=row(d),
        out_shape=jax.ShapeDtypeStruct((m, d), F32),
        compiler_params=_cparams("parallel"),
        name="mix",
    )(x, attn, y_ssm, w_gate, b_gate, w_a, w_m, w_o, ln_g, ln_b)


def _top_values(x, n, with_rank=False):
    vals = []
    rank = jnp.full(x.shape, float(n), F32)
    for i in range(n):
        m = jnp.max(x, axis=0, keepdims=True)
        vals.append(m)
        hit = x == m
        if with_rank:
            rank = jnp.where(hit, float(i), rank)
        x = jnp.where(hit, -jnp.inf, x)
    return (vals, rank) if with_rank else vals


def _gelu(x):
    return 0.5 * x * (1.0 + lax.erf(x * (2.0 ** -0.5)))


def _peer_kernel(h_ref, wqt_ref, k1_ref, k2_ref, u_ref, vt_ref, g_ref, b_ref, o_ref,
                 ht_scr, nsel_scr, rank_scr, e1_scr, e2_scr, act_scr, wa_scr, acc_scr, *, alpha, i1_per_step):
    s = pl.program_id(1)
    n_blocks = pl.num_programs(1) - 1
    half = PEER_QDIM // 2
    nk = PEER_NKEYS

    @pl.when(s == 0)
    def _():
        ht = h_ref[...].T.astype(BF16)
        ht_scr[...] = ht
        k1 = k1_ref[...].astype(BF16)
        k2 = k2_ref[...].astype(BF16)
        for hd in range(PEER_HEADS):
            qt = _dot(wqt_ref[hd * PEER_QDIM:(hd + 1) * PEER_QDIM, :], ht).astype(BF16)
            s1 = _dot(k1, qt[:half])
            s2 = _dot(k2, qt[half:])
            v1 = _top_values(s1, PEER_TOPK)
            v2, rank2 = _top_values(s2, PEER_TOPK, with_rank=True)
            cand = jnp.concatenate(
                [v1[i] + v2[jj] for i in range(PEER_TOPK) for jj in range(PEER_TOPK // (i + 1))], axis=0)
            top = _top_values(cand, PEER_TOPK)
            tau = top[PEER_TOPK - 1]
            zsum = sum(jnp.exp(t - top[0]) for t in top)
            n_sel = jnp.zeros(s1.shape, F32)
            for jj in range(PEER_TOPK):
                n_sel = jnp.where(s1 + v2[jj] >= tau, float(jj + 1), n_sel)
            nsel_scr[hd] = n_sel
            rank_scr[hd] = rank2.astype(BF16)
            e1_scr[hd] = jnp.exp(s1 - v1[0])
            e2_scr[hd] = (jnp.exp(s2 - v2[0]) / zsum).astype(BF16)
        acc_scr[...] = jnp.zeros(acc_scr.shape, F32)
        act_scr[...] = jnp.zeros(act_scr.shape, F32)

    prev = jnp.maximum(s - 1, 0)
    tt = acc_scr.shape[1]
    n_cols = tt // LANES
    dk = ht_scr.shape[0] // n_cols
    rb = 4 * SUBLANES

    def u_chunk(c):
        return _dot(u_ref[:, c * dk:(c + 1) * dk], ht_scr[c * dk:(c + 1) * dk, :])

    def i1_rows(scr):
        return [[scr[hd, pl.ds(prev * i1_per_step + ii, 1), :].astype(BF16) for ii in range(i1_per_step)]
                for hd in range(PEER_HEADS)]

    nsel_rows = i1_rows(nsel_scr)
    cw_rows = i1_rows(e1_scr)

    def gate_cols(c):
        cols = slice(c * LANES, (c + 1) * LANES)
        for r0 in range(0, nk, rb):
            w = [jnp.zeros((rb, LANES), BF16) for _ in range(i1_per_step)]
            for hd in range(PEER_HEADS):
                rk = rank_scr[hd, r0:r0 + rb, cols]
                e2t = e2_scr[hd, r0:r0 + rb, cols]
                for ii in range(i1_per_step):
                    ns = nsel_rows[hd][ii][:, cols]
                    cw = cw_rows[hd][ii][:, cols]
                    w[ii] = w[ii] + jnp.where(rk < ns, e2t * cw, jnp.zeros((), BF16))
            for ii in range(i1_per_step):
                rows = slice(ii * nk + r0, ii * nk + r0 + rb)
                wa_scr[rows, cols] = w[ii] * act_scr[(s + 1) % 2, rows, cols].astype(BF16)

    def second(c0, c1):
        cols = slice(c0 * LANES, c1 * LANES)
        acc_scr[:, cols] += _dot(vt_ref[...], wa_scr[:, cols])

    pre = None
    group = 2 if n_cols % 2 == 0 else 1
    for c in range(n_cols):
        uc = u_chunk(c)
        pre = uc if pre is None else pre + uc
        gate_cols(c)
        if (c + 1) % group == 0:
            second(c + 1 - group, c + 1)
    act_scr[s % 2] = _gelu(pre)

    @pl.when(s == pl.num_programs(1) - 1)
    def _():
        hblk = h_ref[...]
        o_ref[...] = _layer_norm(alpha * hblk + acc_scr[...].T, g_ref[...], b_ref[...])


def _peer(h, wq_t, keys1, keys2, u, v_t, ln_g, ln_b, *, alpha, tt, i1_per_step):
    m, d = h.shape
    n_exp = u.shape[0]
    eb = i1_per_step * PEER_NKEYS
    n_blocks = n_exp // eb
    par = lambda a: pl.BlockSpec(a.shape, lambda i, s: (0, 0))
    scr = lambda dt: pltpu.VMEM((PEER_HEADS, PEER_NKEYS, tt), dt)
    return pl.pallas_call(
        functools.partial(_peer_kernel, alpha=alpha, i1_per_step=i1_per_step),
        grid=(m // tt, n_blocks + 1),
        in_specs=[pl.BlockSpec((tt, d), lambda i, s: (i, 0)), par(wq_t), par(keys1), par(keys2),
                  pl.BlockSpec((eb, d), lambda i, s: (jnp.minimum(s, n_blocks - 1), 0)),
                  pl.BlockSpec((d, eb), lambda i, s: (0, jnp.maximum(s - 1, 0))),
                  par(ln_g), par(ln_b)],
        out_specs=pl.BlockSpec((tt, d), lambda i, s: (i, 0)),
        out_shape=jax.ShapeDtypeStruct((m, d), F32),
        scratch_shapes=[pltpu.VMEM((d, tt), BF16), scr(F32), scr(BF16), scr(F32), scr(BF16),
                        pltpu.VMEM((2, eb, tt), F32), pltpu.VMEM((eb, tt), BF16), pltpu.VMEM((d, tt), F32)],
        compiler_params=_cparams("parallel", "arbitrary"),
        name="peer",
    )(h, wq_t, keys1, keys2, u, v_t, ln_g, ln_b)


def _transpose_cast_kernel(x_ref, o_ref):
    o_ref[...] = x_ref[...].T.astype(o_ref.dtype)


def _transpose_cast(x, dtype, tr):
    r, c = x.shape
    return pl.pallas_call(
        _transpose_cast_kernel,
        grid=(r // tr,),
        in_specs=[pl.BlockSpec((tr, c), lambda i: (i, 0))],
        out_specs=pl.BlockSpec((c, tr), lambda i: (0, i)),
        out_shape=jax.ShapeDtypeStruct((c, r), dtype),
        compiler_params=_cparams("parallel"),
        name="transpose_cast",
    )(x)


def _pad_cols(w, width):
    return jnp.pad(w, ((0, 0), (0, width - w.shape[1])))


def _split_w_in_kernel(w_ref, a_ref, xbc_ref, z_ref, g_ref, *, offs):
    lead = offs[5]
    wi_at = -(-lead // LANES) * LANES
    dt_at = wi_at + LANES
    a_ref[...] = jnp.zeros(a_ref.shape, BF16)
    a_ref[:, 0:lead] = w_ref[:, 0:lead].astype(BF16)
    a_ref[:, wi_at:wi_at + offs[6] - offs[5]] = w_ref[:, offs[5]:offs[6]].astype(BF16)
    a_ref[:, dt_at:dt_at + offs[9] - offs[8]] = w_ref[:, offs[8]:offs[9]].astype(BF16)
    z_ref[...] = w_ref[:, offs[6]:offs[7]].astype(BF16)
    xbc_ref[...] = w_ref[:, offs[7]:offs[8]].astype(BF16)
    g_ref[...] = w_ref[:, offs[9]:offs[10]].astype(BF16)


def _split_w_in(w_in, d_model, d_inner, conv_dim, n_ssm_heads):
    att_w = ATT_HEADS * ATT_HEAD_DIM
    kv_w = ATT_KV_HEADS * ATT_HEAD_DIM
    splits = (att_w, kv_w, kv_w, IDX_HEADS * IDX_DIM, IDX_DIM, IDX_HEADS,
              d_inner, conv_dim, n_ssm_heads, N_BRANCHES * d_model)
    offs = tuple(int(o) for o in np.cumsum((0,) + splits))
    d, n = w_in.shape
    a_w = -(-offs[5] // LANES) * LANES + 2 * LANES
    widths = (a_w, conv_dim, d_inner, N_BRANCHES * d_model)
    tr = LANES
    return pl.pallas_call(
        functools.partial(_split_w_in_kernel, offs=offs),
        grid=(d // tr,),
        in_specs=[pl.BlockSpec((tr, n), lambda i: (i, 0))],
        out_specs=[pl.BlockSpec((tr, w), lambda i: (i, 0)) for w in widths],
        out_shape=[jax.ShapeDtypeStruct((d, w), BF16) for w in widths],
        compiler_params=_cparams("parallel"),
        name="split_w_in",
    )(w_in)


def _tiles(m, n_keys):
    kt = min((512, 384, 256), key=lambda t: (-(-n_keys // t) * t, -t))
    return min(m, 512), min(m, 1024), min(m, 512), kt


def _stream(x, pos, past, wts):
    (w_a, w_xbc, w_z, w_gate, b_gate, conv_w, conv_b, dt_bias, a_log, d_skip_row, norm_w, w_att, w_ssm, w_out,
     ln1_g, ln1_b, wq_t, keys1, keys2, u_exp, v_exp, ln2_g, ln2_b, alpha) = wts
    b, l, d = x.shape
    m = b * l
    xf = x.reshape(m, d)
    conv_dim = w_xbc.shape[1]
    d_inner = norm_w.shape[1]
    tm, tm_mm, tt, kt = _tiles(m, l if past is None else past[0].shape[1] + l)

    tables = _rope_tables(jnp.asarray(np.tile(pos, max(1, tm // l))))
    q, k, v, qi, ki, wi, dt = _proj_a(xf, w_a, tables, tm)
    xbc = _matmul(xf, w_xbc, tm_mm, 1024)
    z = _matmul(xf, w_z, tm_mm, 1024)

    r3 = lambda a: a.reshape(b, l, a.shape[1])
    if past is None:
        k_all, v_all, ki_all = r3(k), r3(v), r3(ki)
        n_valid = l
        k_sel = min(TOPK_MAX, l // 4)
        conv_prev = jnp.zeros((b, SUBLANES, conv_dim), F32)
        ssm_prev = jnp.zeros((b, d_inner, SSM_D_STATE), F32)
    else:
        ck, cv, cki, conv_state, ssm_state = past
        n_past = ck.shape[1]
        n_valid = n_past + l
        n_pad = -n_valid % kt
        cat = lambda c, new: jnp.pad(jnp.concatenate([c.reshape(b, n_past, -1), r3(new)], axis=1),
                                     ((0, 0), (0, n_pad), (0, 0)))
        k_all, v_all, ki_all = cat(ck, k), cat(cv, v), cat(cki, ki)
        k_sel = min(TOPK_MAX, n_valid // 4)
        conv_prev = jnp.pad(conv_state, ((0, 0), (SUBLANES - (CONV_WIDTH - 1), 0), (0, 0)))
        ssm_prev = ssm_state.reshape(b, d_inner, SSM_D_STATE)
    q_pad = -l % LANES
    padq = lambda a: jnp.pad(r3(a), ((0, 0), (0, q_pad), (0, 0)))
    attn = _dsa(padq(q), padq(qi), padq(wi), k_all, v_all, ki_all,
                kt=kt, n_valid=n_valid, q_pos0=int(pos[0]), k_sel=k_sel)[:, :l]

    l_pad = -l % SSD_CHUNK
    padl = lambda a: jnp.pad(r3(a), ((0, 0), (0, l_pad), (0, 0)))
    y_ssm, conv_new, ssm_new = _ssd(padl(xbc), padl(z), padl(dt), conv_prev, ssm_prev,
                                    conv_w, conv_b, dt_bias, a_log, d_skip_row, norm_w,
                                    l_valid=SSD_CHUNK if l_pad == 0 else l)
    y_ssm = y_ssm[:, :l].reshape(m, d_inner)
    conv_new = conv_new[:, SUBLANES - (CONV_WIDTH - 1):]
    n_ssm_heads = d_inner // SSM_HEAD_DIM
    ssm_new = ssm_new.reshape(b, n_ssm_heads, SSM_HEAD_DIM, SSM_D_STATE)

    h1 = _mix(xf, attn.reshape(m, -1), y_ssm, w_gate, b_gate, w_att, w_ssm, w_out, ln1_g, ln1_b,
              alpha=alpha, tm=tm)
    y = _peer(h1, wq_t, keys1, keys2, u_exp, v_exp, ln2_g, ln2_b, alpha=alpha, tt=tt, i1_per_step=4)
    state = (r3(k).reshape(b, l, ATT_KV_HEADS, ATT_HEAD_DIM), r3(v).reshape(b, l, ATT_KV_HEADS, ATT_HEAD_DIM),
             r3(ki), conv_new, ssm_new)
    return y.reshape(b, l, d), state


def kernel(x_prompt, x_sample, cache_k, cache_v, cache_kidx, state_conv, state_ssm, w_in, b_gate, conv_w, conv_b, dt_bias, a_log, d_skip, ssm_norm_w, w_attn_br, w_ssm_br, w_out, ln1_g, ln1_b, peer_wq, peer_keys1, peer_keys2, peer_u, peer_v, ln2_g, ln2_b):
    depth = w_in.shape[0]
    d_model = x_prompt.shape[2]
    d_inner = ssm_norm_w.shape[1]
    conv_dim = conv_w.shape[2]
    n_ssm_heads = a_log.shape[1]
    alpha = (2.0 * depth) ** 0.25
    pos_p = np.arange(x_prompt.shape[1])
    pos_s = cache_k.shape[2] + np.arange(x_sample.shape[1])

    hp, hs = x_prompt, x_sample
    new_p, new_s = [], []
    for l in range(depth):
        w_a, w_xbc, w_z, w_gate = _split_w_in(w_in[l], d_model, d_inner, conv_dim, n_ssm_heads)
        row = lambda a: a.reshape(1, -1)
        wts = (w_a, w_xbc, w_z, w_gate, row(b_gate[l]),
               jnp.pad(conv_w[l], ((0, SUBLANES - CONV_WIDTH), (0, 0))), row(conv_b[l]),
               _pad_cols(row(dt_bias[l]), LANES), _pad_cols(row(a_log[l]), LANES),
               row(jnp.repeat(d_skip[l], SSM_HEAD_DIM)), row(ssm_norm_w[l]),
               w_attn_br[l].astype(BF16), w_ssm_br[l].astype(BF16), w_out[l].astype(BF16),
               row(ln1_g[l]), row(ln1_b[l]),
               _transpose_cast(peer_wq[l], BF16, 512), peer_keys1[l], peer_keys2[l],
               peer_u[l].astype(BF16), _transpose_cast(peer_v[l], BF16, 512), row(ln2_g[l]), row(ln2_b[l]), alpha)
        hp, sp = _stream(hp, pos_p, None, wts)
        past = (cache_k[l], cache_v[l], cache_kidx[l], state_conv[l], state_ssm[l])
        hs, ss = _stream(hs, pos_s, past, wts)
        new_p.append(sp)
        new_s.append(ss)

    stack = lambda lst, i: jnp.stack([e[i] for e in lst], axis=0)
    return (hp, hs) + tuple(stack(new_p, i) for i in range(5)) + tuple(stack(new_s, i) for i in range(5))
```

```python
import functools

import jax
import jax.numpy as jnp
import numpy as np
from jax import lax
from jax.experimental import pallas as pl
from jax.experimental.pallas import tpu as pltpu

F32 = jnp.float32
BF16 = jnp.bfloat16

LANES = 128
SUBLANES = 8
VMEM_LIMIT_BYTES = 48 * 1024 * 1024

CHUNK = 64
ATT_HEADS = 16
ATT_HEAD_DIM = 64
ATT_KV_HEADS = 4
ATT_GROUP = ATT_HEADS // ATT_KV_HEADS
IDX_HEADS = 4
IDX_DIM = 64
TOPK_MAX = 256
ROPE_THETA = 500000.0
ROPE_FRACTION = 4
SSM_HEAD_DIM = 64
SSM_GROUPS = 4
SSM_D_STATE = 128
CONV_WIDTH = 4
N_BRANCHES = 2
PEER_HEADS = 8
PEER_NKEYS = 128
PEER_QDIM = 256
PEER_TOPK = 16
LN_EPS = 1e-5
RMS_EPS = 1e-5
NEG_INF = -1e30
SSD_CHUNK = 128
LOG2_E = 1.4426950408889634
INT32_MIN = -(2 ** 31)
CODE_NEG_INFINITY = INT32_MIN + 0x7FFFFF


def _cparams(*sem):
    return pltpu.CompilerParams(dimension_semantics=sem, vmem_limit_bytes=VMEM_LIMIT_BYTES)


def _dot(a, b):
    return jnp.dot(a, b, preferred_element_type=F32)


def _dot_nt(a, b):
    return lax.dot_general(a, b, (((1,), (1,)), ((), ())), preferred_element_type=F32)


def _rope_tables(pos):
    rd = ATT_HEAD_DIM // ROPE_FRACTION
    half = rd // 2
    inv = ROPE_THETA ** (-(jnp.arange(half, dtype=F32) * 2.0) / rd)
    ang = pos.astype(F32)[:, None] * inv[None, :]
    cos, sin = jnp.cos(ang), jnp.sin(ang)
    n = pos.shape[0]
    ones = jnp.ones((n, ATT_HEAD_DIM - rd), F32)
    zeros = jnp.zeros((n, ATT_HEAD_DIM - rd), F32)
    zh = jnp.zeros((n, half), F32)
    cos_h = jnp.concatenate([cos, cos, ones], axis=1)
    sa_h = jnp.concatenate([-sin, zh, zeros], axis=1)
    sb_h = jnp.concatenate([zh, sin, zeros], axis=1)
    rep = LANES // ATT_HEAD_DIM
    return jnp.tile(cos_h, (1, rep)), jnp.tile(sa_h, (1, rep)), jnp.tile(sb_h, (1, rep))


def _proj_a_kernel(x_ref, w_ref, cos_ref, sa_ref, sb_ref,
                   q_ref, k_ref, v_ref, qi_ref, ki_ref, wi_ref, dt_ref):
    x = x_ref[...].astype(BF16)
    cos, sa, sb = cos_ref[...], sa_ref[...], sb_ref[...]
    half = ATT_HEAD_DIM // ROPE_FRACTION // 2

    def rope(t):
        up = pltpu.roll(t, LANES - half, 1)
        down = pltpu.roll(t, half, 1)
        return t * cos + up * sa + down * sb

    def tiles(c0, n):
        out = []
        for c in range(c0, c0 + n, 2):
            w = min(2, c0 + n - c)
            out += _lane_tiles(_dot(x, w_ref[:, c * LANES:(c + w) * LANES]))
        return out

    nq = q_ref.shape[1] // LANES
    nk = k_ref.shape[1] // LANES
    nqi = qi_ref.shape[1] // LANES
    c = 0
    for ref, n, rotary in ((q_ref, nq, True), (k_ref, nk, True), (v_ref, nk, False), (qi_ref, nqi, True)):
        for j, t in enumerate(tiles(c, n)):
            ref[:, j * LANES:(j + 1) * LANES] = rope(t) if rotary else t
        c += n
    ki, wi, dt = tiles(c, 3)
    ki_ref[...] = rope(ki)[:, :IDX_DIM]
    wi_ref[...] = wi
    dt_ref[...] = dt


def _proj_a(x, w_a, tables, tm):
    m, d = x.shape
    att_w = ATT_HEADS * ATT_HEAD_DIM
    kv_w = ATT_KV_HEADS * ATT_HEAD_DIM
    idx_w = IDX_HEADS * IDX_DIM
    tab_blocks = tables[0].shape[0] // tm
    row = lambda i: (i, 0)
    tab = lambda i: (i % tab_blocks, 0)
    full = lambda i: (0, 0)
    out_w = (att_w, kv_w, kv_w, idx_w, IDX_DIM, LANES, LANES)
    return pl.pallas_call(
        _proj_a_kernel,
        grid=(m // tm,),
        in_specs=[pl.BlockSpec((tm, d), row), pl.BlockSpec(w_a.shape, full)]
                 + [pl.BlockSpec((tm, LANES), tab)] * 3,
        out_specs=[pl.BlockSpec((tm, w), row) for w in out_w],
        out_shape=[jax.ShapeDtypeStruct((m, w), F32) for w in out_w],
        compiler_params=_cparams("parallel"),
        name="proj_a",
    )(x, w_a, *tables)


def _matmul_kernel(x_ref, w_ref, o_ref):
    o_ref[...] = _dot(x_ref[...].astype(BF16), w_ref[...])


def _matmul(x, w, tm, tn):
    m, k = x.shape
    n = w.shape[1]
    return pl.pallas_call(
        _matmul_kernel,
        grid=(m // tm, n // tn),
        in_specs=[pl.BlockSpec((tm, k), lambda i, j: (i, 0)), pl.BlockSpec((k, tn), lambda i, j: (0, j))],
        out_specs=pl.BlockSpec((tm, tn), lambda i, j: (i, j)),
        out_shape=jax.ShapeDtypeStruct((m, n), F32),
        compiler_params=_cparams("parallel", "arbitrary"),
        name="matmul",
    )(x, w)


def _tree(op, xs):
    xs = list(xs)
    while len(xs) > 1:
        xs = [op(xs[i], xs[i + 1]) for i in range(0, len(xs) - 1, 2)] + ([xs[-1]] if len(xs) % 2 else [])
    return xs[0]


def _lane_tiles(x):
    return [x[:, c * LANES:(c + 1) * LANES] for c in range(x.shape[1] // LANES)]


def _dsa_kernel(q_ref, qi_ref, wi_ref, k_ref, v_ref, ki_ref, earlier_ref, o_ref,
                score_scr, sel_scr, lg_scr, q_scr, m_scr, l_scr, acc_scr,
                *, kt, nkt, n_valid, q_pos0, k_sel):
    qb = LANES
    j = pl.program_id(1)
    q_first = q_pos0 + j * qb
    nt = jnp.minimum(nkt, (q_first + qb + kt - 1) // kt)

    key_off = lax.broadcasted_iota(jnp.int32, (kt, qb), 0)
    q_chunk = (q_first + lax.broadcasted_iota(jnp.int32, (1, qb), 1)) // CHUNK
    key_limit = jnp.minimum((q_chunk + 1) * CHUNK, n_valid)

    def admissible(t):
        return key_off < key_limit - t * kt

    qi_t = qi_ref[...].T.astype(BF16)
    wi_t = wi_ref[...].T

    def score_tile(t, carry):
        ki_t = ki_ref[pl.ds(pl.multiple_of(t * kt, LANES), kt), :].astype(BF16)
        s = jnp.zeros((kt, qb), F32)
        for h in range(IDX_HEADS):
            lg = _dot(ki_t, qi_t[h * IDX_DIM:(h + 1) * IDX_DIM, :])
            s = s + jnp.maximum(lg, 0.0) * wi_t[h:h + 1, :]
        s = jnp.where(s == 0.0, 0.0, s)
        score_scr[t] = jnp.where(admissible(t), s, NEG_INF)
        return carry

    lax.fori_loop(0, nt, score_tile, 0)

    def count(pred):
        def body(t, acc):
            hit = jnp.where(pred(score_scr[t]), 1.0, 0.0)
            return acc + _tree(jnp.add, [hit[r:r + SUBLANES] for r in range(0, kt, SUBLANES)])
        acc = lax.fori_loop(0, nt, body, jnp.zeros((SUBLANES, qb), F32))
        return jnp.sum(acc, axis=0, keepdims=True)

    kf = jnp.float32(k_sel)

    def decode(code):
        bits = jnp.where(code < 0, code ^ jnp.int32(0x7FFFFFFF), code)
        return lax.bitcast_convert_type(bits, F32)

    def at_least_k(code):
        thr_f = decode(code)
        return jnp.logical_or(count(lambda sc: sc >= thr_f) >= kf, code < CODE_NEG_INFINITY)

    zero = jnp.zeros((1, qb), jnp.int32)
    code = jnp.where(count(lambda sc: sc >= 0.0) >= kf, zero, jnp.int32(INT32_MIN))

    def bit_pass(i, code):
        cand = code | jnp.left_shift(jnp.int32(1), 30 - i)
        return jnp.where(at_least_k(cand), cand, code)

    thr = decode(lax.fori_loop(0, 31, bit_pass, code))

    need = kf - count(lambda sc: sc > thr)
    earlier = earlier_ref[...]

    def select_tile(t, run):
        sc = score_scr[t]
        tie = sc == thr
        eq = jnp.where(tie, 1.0, 0.0)
        rank = _dot(earlier, eq.astype(BF16)) + run
        take = jnp.logical_or(sc > thr, jnp.logical_and(tie, rank < need))
        sel = jnp.where(jnp.logical_and(take, admissible(t)), 1.0, 0.0)
        sel_scr[t] = sel.T
        return run + jnp.sum(eq, axis=0, keepdims=True)

    lax.fori_loop(0, nt, select_tile, jnp.zeros((1, qb), F32))

    scale = ATT_HEAD_DIM ** -0.5 * LOG2_E
    hd = ATT_HEAD_DIM
    rows = ATT_GROUP * qb
    pair = LANES // hd
    for g in range(ATT_KV_HEADS):
        qg = jnp.concatenate(
            [q_ref[:, (g * ATT_GROUP + r) * hd:(g * ATT_GROUP + r + 1) * hd] for r in range(ATT_GROUP)],
            axis=0) * scale
        zeros = jnp.zeros_like(qg)
        parts = [qg if c == g % pair else zeros for c in range(pair)]
        q_scr[g] = jnp.concatenate(parts, axis=1).astype(BF16)
    m_scr[...] = jnp.full(m_scr.shape, NEG_INF, F32)
    l_scr[...] = jnp.zeros(l_scr.shape, F32)
    acc_scr[...] = jnp.zeros(acc_scr.shape, F32)

    def kv_tile(ref, t, g):
        lane0 = (g // pair) * LANES
        return ref[pl.ds(pl.multiple_of(t * kt, LANES), kt), lane0:lane0 + LANES].astype(BF16)

    def logits_tile(t, carry):
        sel = sel_scr[t][None] > 0.0
        for g in range(ATT_KV_HEADS):
            lg = _dot_nt(q_scr[g], kv_tile(k_ref, t, g)).reshape(ATT_GROUP, qb, kt)
            lg = jnp.where(sel, lg, NEG_INF).reshape(rows, kt)
            lg_scr[t, g] = lg
            m_scr[g] = jnp.maximum(m_scr[g], _tree(jnp.maximum, _lane_tiles(lg)))
        return carry

    lax.fori_loop(0, nt, logits_tile, 0)
    for g in range(ATT_KV_HEADS):
        m_scr[g] = jnp.broadcast_to(jnp.max(m_scr[g], axis=1, keepdims=True), (rows, LANES))

    def pv_tile(t, carry):
        for g in range(ATT_KV_HEADS):
            m = m_scr[g]
            ps = [jnp.exp2(lg_c - m) for lg_c in _lane_tiles(lg_scr[t, g])]
            l_scr[g] += _tree(jnp.add, ps)
            acc_scr[g] += _dot(jnp.concatenate(ps, axis=1).astype(BF16), kv_tile(v_ref, t, g))
        return carry

    lax.fori_loop(0, nt, pv_tile, 0)
    for g in range(ATT_KV_HEADS):
        denom = jnp.sum(l_scr[g], axis=1, keepdims=True)
        out = acc_scr[g][:, (g % pair) * hd:(g % pair + 1) * hd] / denom
        for r in range(ATT_GROUP):
            h = g * ATT_GROUP + r
            o_ref[:, h * hd:(h + 1) * hd] = out[r * qb:(r + 1) * qb]


def _dsa(q, qi, wi, k, v, ki, *, kt, n_valid, q_pos0, k_sel):
    b, lq, _ = q.shape
    nk = k.shape[1]
    nkt = nk // kt
    qb = LANES
    qspec = lambda w: pl.BlockSpec((None, qb, w), lambda bi, j: (bi, j, 0))
    kspec = lambda w: pl.BlockSpec((None, nk, w), lambda bi, j: (bi, 0, 0))
    rows = ATT_GROUP * qb
    assert kt >= k_sel, "a query block must see at least k_sel key slots"
    earlier = jnp.tril(jnp.ones((kt, kt), BF16), -1)
    kern = functools.partial(_dsa_kernel, kt=kt, nkt=nkt, n_valid=n_valid, q_pos0=q_pos0, k_sel=k_sel)
    return pl.pallas_call(
        kern,
        grid=(b, lq // qb),
        in_specs=[qspec(q.shape[2]), qspec(qi.shape[2]), qspec(wi.shape[2]),
                  kspec(k.shape[2]), kspec(v.shape[2]), kspec(ki.shape[2]),
                  pl.BlockSpec((kt, kt), lambda bi, j: (0, 0))],
        out_specs=qspec(q.shape[2]),
        out_shape=jax.ShapeDtypeStruct(q.shape, F32),
        scratch_shapes=[
            pltpu.VMEM((nkt, kt, qb), F32),
            pltpu.VMEM((nkt, qb, kt), F32),
            pltpu.VMEM((nkt, ATT_KV_HEADS, rows, kt), F32),
            pltpu.VMEM((ATT_KV_HEADS, rows, LANES), BF16),
            pltpu.VMEM((ATT_KV_HEADS, rows, LANES), F32),
            pltpu.VMEM((ATT_KV_HEADS, rows, LANES), F32),
            pltpu.VMEM((ATT_KV_HEADS, rows, LANES), F32),
        ],
        compiler_params=_cparams("parallel", "arbitrary"),
        name="dsa",
    )(q, qi, wi, k, v, ki, earlier)


def _silu(x):
    return x / (1.0 + jnp.exp(-x))


def _ssd_kernel(xbc_ref, z_ref, dt_ref, cprev_ref, sprev_ref, cw_ref, cb_ref, dtb_ref, alog_ref,
                dsk_ref, nw_ref, y_ref, cnew_ref, s_ref, buf_ref, yh_ref, *, l_valid):
    c = pl.program_id(1)
    lc = SSD_CHUNK
    hist = SUBLANES
    d_inner = y_ref.shape[1]
    n_heads = d_inner // SSM_HEAD_DIM
    gn = SSM_GROUPS * SSM_D_STATE
    heads_per_group = n_heads // SSM_GROUPS
    p = SSM_HEAD_DIM

    @pl.when(c == 0)
    def _():
        buf_ref[0:hist, :] = cprev_ref[...]
        s_ref[...] = sprev_ref[...]

    buf_ref[hist:hist + lc, :] = xbc_ref[...]
    conv = cb_ref[...] + sum(
        buf_ref[hist - (CONV_WIDTH - 1) + t:hist - (CONV_WIDTH - 1) + t + lc, :] * cw_ref[t:t + 1, :]
        for t in range(CONV_WIDTH))
    xbc = _silu(conv)
    cnew_ref[...] = buf_ref[l_valid:l_valid + hist, :]
    buf_ref[0:hist, :] = buf_ref[lc:lc + hist, :]

    xs = xbc[:, :d_inner]
    b_all = xbc[:, d_inner:d_inner + gn].astype(BF16)
    c_all = xbc[:, d_inner + gn:].astype(BF16)

    row = lax.broadcasted_iota(jnp.int32, (lc, LANES), 0)
    pre = dt_ref[...] + dtb_ref[...]
    dt = jnp.maximum(pre, 0.0) + jnp.log1p(jnp.exp(-jnp.abs(pre)))
    dt = jnp.where(row < l_valid, dt, 0.0)
    a = dt * (-jnp.exp(alog_ref[...]))
    ta = lax.broadcasted_iota(jnp.int32, (lc, lc), 0)
    tb = lax.broadcasted_iota(jnp.int32, (lc, lc), 1)
    tril = ta >= tb
    acum = jnp.dot(jnp.where(tril, 1.0, 0.0), a, preferred_element_type=F32,
                   precision=lax.Precision.HIGHEST)
    acum_t = acum.T
    dt_t = dt.T
    xs_t = xs.T

    for g in range(SSM_GROUPS):
        b_g = b_all[:, g * SSM_D_STATE:(g + 1) * SSM_D_STATE]
        c_g = c_all[:, g * SSM_D_STATE:(g + 1) * SSM_D_STATE]
        cb = _dot_nt(c_g, b_g)
        for hh in range(heads_per_group):
            h = g * heads_per_group + hh
            col = acum[:, h:h + 1]
            arow = acum_t[h:h + 1, :]
            dtrow = dt_t[h:h + 1, :]
            alast = acum_t[h:h + 1, lc - 1:lc]
            decay = jnp.where(tril, jnp.exp(jnp.where(tril, col - arow, 0.0)), 0.0)
            mh = (cb * decay * dtrow).astype(BF16)
            y_diag = _dot(mh, xs[:, h * p:(h + 1) * p].astype(BF16))
            s_h = s_ref[h * p:(h + 1) * p, :]
            y_off = _dot_nt(c_g, s_h.astype(BF16)) * jnp.exp(col)
            yh_ref[:, h * p:(h + 1) * p] = y_diag + y_off
            w_row = dtrow * jnp.exp(alast - arow)
            x_t = (xs_t[h * p:(h + 1) * p, :] * w_row).astype(BF16)
            s_ref[h * p:(h + 1) * p, :] = s_h * jnp.exp(alast) + _dot(x_t, b_g)

    y = yh_ref[...] + dsk_ref[...] * xs
    y = y * _silu(z_ref[...])
    gw = d_inner // SSM_GROUPS
    for g in range(SSM_GROUPS):
        yg = y[:, g * gw:(g + 1) * gw]
        ms = jnp.mean(yg * yg, axis=1, keepdims=True)
        y_ref[:, g * gw:(g + 1) * gw] = yg * lax.rsqrt(ms + RMS_EPS) * nw_ref[:, g * gw:(g + 1) * gw]


def _ssd(xbc, z_src, dt, conv_prev, ssm_prev, conv_w, conv_b, dt_bias, a_log, d_skip, norm_w, *, l_valid):
    b, l, conv_dim = xbc.shape
    d_inner = norm_w.shape[1]
    nc = l // SSD_CHUNK
    tok = lambda w: pl.BlockSpec((None, SSD_CHUNK, w), lambda bi, c: (bi, c, 0))
    per_b = lambda s: pl.BlockSpec((None,) + s, lambda bi, c: (bi, 0, 0))
    par = lambda a: pl.BlockSpec(a.shape, lambda bi, c: (0, 0))
    kern = functools.partial(_ssd_kernel, l_valid=l_valid)
    return pl.pallas_call(
        kern,
        grid=(b, nc),
        in_specs=[tok(conv_dim), tok(d_inner), tok(LANES),
                  per_b((SUBLANES, conv_dim)), per_b(ssm_prev.shape[1:]),
                  par(conv_w), par(conv_b), par(dt_bias), par(a_log), par(d_skip), par(norm_w)],
        out_specs=[tok(d_inner), per_b((SUBLANES, conv_dim)), per_b(ssm_prev.shape[1:])],
        out_shape=[jax.ShapeDtypeStruct((b, l, d_inner), F32),
                   jax.ShapeDtypeStruct((b, SUBLANES, conv_dim), F32),
                   jax.ShapeDtypeStruct(ssm_prev.shape, F32)],
        scratch_shapes=[pltpu.VMEM((SSD_CHUNK + 2 * SUBLANES, conv_dim), F32),
                        pltpu.VMEM((SSD_CHUNK, d_inner), F32)],
        compiler_params=_cparams("parallel", "arbitrary"),
        name="ssd",
    )(xbc, z_src, dt, conv_prev, ssm_prev, conv_w, conv_b, dt_bias, a_log, d_skip, norm_w)


def _layer_norm(x, g, b):
    mu = jnp.mean(x, axis=-1, keepdims=True)
    xc = x - mu
    var = jnp.mean(xc * xc, axis=-1, keepdims=True)
    return xc * lax.rsqrt(var + LN_EPS) * g + b


def _mix_kernel(x_ref, attn_ref, yssm_ref, wg_ref, bg_ref, wa_ref, wm_ref, wo_ref, g_ref, b_ref, o_ref,
                *, alpha):
    d = x_ref.shape[1]
    x = x_ref[...]
    ya = _dot(attn_ref[...].astype(BF16), wa_ref[...])
    ym = _dot(yssm_ref[...].astype(BF16), wm_ref[...])
    gates = _dot(x.astype(BF16), wg_ref[...]) + bg_ref[...]
    gates = 1.0 / (1.0 + jnp.exp(-gates))
    mixed = gates[:, :d] * ya + gates[:, d:] * ym
    out = _dot(mixed.astype(BF16), wo_ref[...])
    o_ref[...] = _layer_norm(alpha * x + out, g_ref[...], b_ref[...])


def _mix(x, attn, y_ssm, w_gate, b_gate, w_a, w_m, w_o, ln_g, ln_b, *, alpha, tm):
    m, d = x.shape
    row = lambda w: pl.BlockSpec((tm, w), lambda i: (i, 0))
    par = lambda a: pl.BlockSpec(a.shape, lambda i: (0, 0))
    return pl.pallas_call(
        functools.partial(_mix_kernel, alpha=alpha),
        grid=(m // tm,),
        in_specs=[row(d), row(attn.shape[1]), row(y_ssm.shape[1]),
                  par(w_gate), par(b_gate), par(w_a), par(w_m), par(w_o), par(ln_g), par(ln_b)],
        out_specs=row(d),
        out_shape=jax.ShapeDtypeStruct((m, d), F32),
        compiler_params=_cparams("parallel"),
        name="mix",
    )(x, attn, y_ssm, w_gate, b_gate, w_a, w_m, w_o, ln_g, ln_b)


def _top_values(x, n, with_rank=False):
    vals = []
    rank = jnp.full(x.shape, float(n), F32)
    for i in range(n):
        m = jnp.max(x, axis=0, keepdims=True)
        vals.append(m)
        hit = x == m
        if with_rank:
            rank = jnp.where(hit, float(i), rank)
        x = jnp.where(hit, -jnp.inf, x)
    return (vals, rank) if with_rank else vals


def _gelu(x):
    return 0.5 * x * (1.0 + lax.erf(x * (2.0 ** -0.5)))


def _peer_kernel(h_ref, wqt_ref, k1_ref, k2_ref, u_ref, vt_ref, g_ref, b_ref, o_ref,
                 ht_scr, nsel_scr, rank_scr, e1_scr, e2_scr, act_scr, wa_scr, acc_scr, *, alpha, i1_per_step):
    s = pl.program_id(1)
    n_blocks = pl.num_programs(1) - 1
    half = PEER_QDIM // 2
    nk = PEER_NKEYS

    @pl.when(s == 0)
    def _():
        ht = h_ref[...].T.astype(BF16)
        ht_scr[...] = ht
        k1 = k1_ref[...].astype(BF16)
        k2 = k2_ref[...].astype(BF16)
        for hd in range(PEER_HEADS):
            qt = _dot(wqt_ref[hd * PEER_QDIM:(hd + 1) * PEER_QDIM, :], ht).astype(BF16)
            s1 = _dot(k1, qt[:half])
            s2 = _dot(k2, qt[half:])
            v1 = _top_values(s1, PEER_TOPK)
            v2, rank2 = _top_values(s2, PEER_TOPK, with_rank=True)
            cand = jnp.concatenate(
                [v1[i] + v2[jj] for i in range(PEER_TOPK) for jj in range(PEER_TOPK // (i + 1))], axis=0)
            top = _top_values(cand, PEER_TOPK)
            tau = top[PEER_TOPK - 1]
            zsum = sum(jnp.exp(t - top[0]) for t in top)
            n_sel = jnp.zeros(s1.shape, F32)
            for jj in range(PEER_TOPK):
                n_sel = jnp.where(s1 + v2[jj] >= tau, float(jj + 1), n_sel)
            nsel_scr[hd] = n_sel
            rank_scr[hd] = rank2.astype(BF16)
            e1_scr[hd] = jnp.exp(s1 - v1[0])
            e2_scr[hd] = (jnp.exp(s2 - v2[0]) / zsum).astype(BF16)
        acc_scr[...] = jnp.zeros(acc_scr.shape, F32)
        act_scr[...] = jnp.zeros(act_scr.shape, F32)

    prev = jnp.maximum(s - 1, 0)
    tt = acc_scr.shape[1]
    n_cols = tt // LANES
    dk = ht_scr.shape[0] // n_cols
    rb = 4 * SUBLANES

    def u_chunk(c):
        return _dot(u_ref[:, c * dk:(c + 1) * dk], ht_scr[c * dk:(c + 1) * dk, :])

    def i1_rows(scr):
        return [[scr[hd, pl.ds(prev * i1_per_step + ii, 1), :].astype(BF16) for ii in range(i1_per_step)]
                for hd in range(PEER_HEADS)]

    nsel_rows = i1_rows(nsel_scr)
    cw_rows = i1_rows(e1_scr)

    def gate_cols(c):
        cols = slice(c * LANES, (c + 1) * LANES)
        for r0 in range(0, nk, rb):
            w = [jnp.zeros((rb, LANES), BF16) for _ in range(i1_per_step)]
            for hd in range(PEER_HEADS):
                rk = rank_scr[hd, r0:r0 + rb, cols]
                e2t = e2_scr[hd, r0:r0 + rb, cols]
                for ii in range(i1_per_step):
                    ns = nsel_rows[hd][ii][:, cols]
                    cw = cw_rows[hd][ii][:, cols]
                    w[ii] = w[ii] + jnp.where(rk < ns, e2t * cw, jnp.zeros((), BF16))
            for ii in range(i1_per_step):
                rows = slice(ii * nk + r0, ii * nk + r0 + rb)
                wa_scr[rows, cols] = w[ii] * act_scr[(s + 1) % 2, rows, cols].astype(BF16)

    def second(c0, c1):
        cols = slice(c0 * LANES, c1 * LANES)
        acc_scr[:, cols] += _dot(vt_ref[...], wa_scr[:, cols])

    pre = None
    group = 2 if n_cols % 2 == 0 else 1
    for c in range(n_cols):
        uc = u_chunk(c)
        pre = uc if pre is None else pre + uc
        gate_cols(c)
        if (c + 1) % group == 0:
            second(c + 1 - group, c + 1)
    act_scr[s % 2] = _gelu(pre)

    @pl.when(s == pl.num_programs(1) - 1)
    def _():
        hblk = h_ref[...]
        o_ref[...] = _layer_norm(alpha * hblk + acc_scr[...].T, g_ref[...], b_ref[...])


def _peer(h, wq_t, keys1, keys2, u, v_t, ln_g, ln_b, *, alpha, tt, i1_per_step):
    m, d = h.shape
    n_exp = u.shape[0]
    eb = i1_per_step * PEER_NKEYS
    n_blocks = n_exp // eb
    par = lambda a: pl.BlockSpec(a.shape, lambda i, s: (0, 0))
    scr = lambda dt: pltpu.VMEM((PEER_HEADS, PEER_NKEYS, tt), dt)
    return pl.pallas_call(
        functools.partial(_peer_kernel, alpha=alpha, i1_per_step=i1_per_step),
        grid=(m // tt, n_blocks + 1),
        in_specs=[pl.BlockSpec((tt, d), lambda i, s: (i, 0)), par(wq_t), par(keys1), par(keys2),
                  pl.BlockSpec((eb, d), lambda i, s: (jnp.minimum(s, n_blocks - 1), 0)),
                  pl.BlockSpec((d, eb), lambda i, s: (0, jnp.maximum(s - 1, 0))),
                  par(ln_g), par(ln_b)],
        out_specs=pl.BlockSpec((tt, d), lambda i, s: (i, 0)),
        out_shape=jax.ShapeDtypeStruct((m, d), F32),
        scratch_shapes=[pltpu.VMEM((d, tt), BF16), scr(F32), scr(BF16), scr(F32), scr(BF16),
                        pltpu.VMEM((2, eb, tt), F32), pltpu.VMEM((eb, tt), BF16), pltpu.VMEM((d, tt), F32)],
        compiler_params=_cparams("parallel", "arbitrary"),
        name="peer",
    )(h, wq_t, keys1, keys2, u, v_t, ln_g, ln_b)


def _transpose_cast_kernel(x_ref, o_ref):
    o_ref[...] = x_ref[...].T.astype(o_ref.dtype)


def _transpose_cast(x, dtype, tr):
    r, c = x.shape
    return pl.pallas_call(
        _transpose_cast_kernel,
        grid=(r // tr,),
        in_specs=[pl.BlockSpec((tr, c), lambda i: (i, 0))],
        out_specs=pl.BlockSpec((c, tr), lambda i: (0, i)),
        out_shape=jax.ShapeDtypeStruct((c, r), dtype),
        compiler_params=_cparams("parallel"),
        name="transpose_cast",
    )(x)


def _pad_cols(w, width):
    return jnp.pad(w, ((0, 0), (0, width - w.shape[1])))


def _split_w_in_kernel(w_ref, a_ref, xbc_ref, z_ref, g_ref, *, offs):
    lead = offs[5]
    wi_at = -(-lead // LANES) * LANES
    dt_at = wi_at + LANES
    a_ref[...] = jnp.zeros(a_ref.shape, BF16)
    a_ref[:, 0:lead] = w_ref[:, 0:lead].astype(BF16)
    a_ref[:, wi_at:wi_at + offs[6] - offs[5]] = w_ref[:, offs[5]:offs[6]].astype(BF16)
    a_ref[:, dt_at:dt_at + offs[9] - offs[8]] = w_ref[:, offs[8]:offs[9]].astype(BF16)
    z_ref[...] = w_ref[:, offs[6]:offs[7]].astype(BF16)
    xbc_ref[...] = w_ref[:, offs[7]:offs[8]].astype(BF16)
    g_ref[...] = w_ref[:, offs[9]:offs[10]].astype(BF16)


def _split_w_in(w_in, d_model, d_inner, conv_dim, n_ssm_heads):
    att_w = ATT_HEADS * ATT_HEAD_DIM
    kv_w = ATT_KV_HEADS * ATT_HEAD_DIM
    splits = (att_w, kv_w, kv_w, IDX_HEADS * IDX_DIM, IDX_DIM, IDX_HEADS,
              d_inner, conv_dim, n_ssm_heads, N_BRANCHES * d_model)
    offs = tuple(int(o) for o in np.cumsum((0,) + splits))
    d, n = w_in.shape
    a_w = -(-offs[5] // LANES) * LANES + 2 * LANES
    widths = (a_w, conv_dim, d_inner, N_BRANCHES * d_model)
    tr = LANES
    return pl.pallas_call(
        functools.partial(_split_w_in_kernel, offs=offs),
        grid=(d // tr,),
        in_specs=[pl.BlockSpec((tr, n), lambda i: (i, 0))],
        out_specs=[pl.BlockSpec((tr, w), lambda i: (i, 0)) for w in widths],
        out_shape=[jax.ShapeDtypeStruct((d, w), BF16) for w in widths],
        compiler_params=_cparams("parallel"),
        name="split_w_in",
    )(w_in)


def _tiles(m, n_keys):
    kt = min((512, 384, 256), key=lambda t: (-(-n_keys // t) * t, -t))
    return min(m, 512), min(m, 1024), min(m, 512), kt


def _stream(x, pos, past, wts):
    (w_a, w_xbc, w_z, w_gate, b_gate, conv_w, conv_b, dt_bias, a_log, d_skip_row, norm_w, w_att, w_ssm, w_out,
     ln1_g, ln1_b, wq_t, keys1, keys2, u_exp, v_exp, ln2_g, ln2_b, alpha) = wts
    b, l, d = x.shape
    m = b * l
    xf = x.reshape(m, d)
    conv_dim = w_xbc.shape[1]
    d_inner = norm_w.shape[1]
    tm, tm_mm, tt, kt = _tiles(m, l if past is None else past[0].shape[1] + l)

    tables = _rope_tables(jnp.asarray(np.tile(pos, max(1, tm // l))))
    q, k, v, qi, ki, wi, dt = _proj_a(xf, w_a, tables, tm)
    xbc = _matmul(xf, w_xbc, tm_mm, 1024)
    z = _matmul(xf, w_z, tm_mm, 1024)

    r3 = lambda a: a.reshape(b, l, a.shape[1])
    if past is None:
        k_all, v_all, ki_all = r3(k), r3(v), r3(ki)
        n_valid = l
        k_sel = min(TOPK_MAX, l // 4)
        conv_prev = jnp.zeros((b, SUBLANES, conv_dim), F32)
        ssm_prev = jnp.zeros((b, d_inner, SSM_D_STATE), F32)
    else:
        ck, cv, cki, conv_state, ssm_state = past
        n_past = ck.shape[1]
        n_valid = n_past + l
        n_pad = -n_valid % kt
        cat = lambda c, new: jnp.pad(jnp.concatenate([c.reshape(b, n_past, -1), r3(new)], axis=1),
                                     ((0, 0), (0, n_pad), (0, 0)))
        k_all, v_all, ki_all = cat(ck, k), cat(cv, v), cat(cki, ki)
        k_sel = min(TOPK_MAX, n_valid // 4)
        conv_prev = jnp.pad(conv_state, ((0, 0), (SUBLANES - (CONV_WIDTH - 1), 0), (0, 0)))
        ssm_prev = ssm_state.reshape(b, d_inner, SSM_D_STATE)
    q_pad = -l % LANES
    padq = lambda a: jnp.pad(r3(a), ((0, 0), (0, q_pad), (0, 0)))
    attn = _dsa(padq(q), padq(qi), padq(wi), k_all, v_all, ki_all,
                kt=kt, n_valid=n_valid, q_pos0=int(pos[0]), k_sel=k_sel)[:, :l]

    l_pad = -l % SSD_CHUNK
    padl = lambda a: jnp.pad(r3(a), ((0, 0), (0, l_pad), (0, 0)))
    y_ssm, conv_new, ssm_new = _ssd(padl(xbc), padl(z), padl(dt), conv_prev, ssm_prev,
                                    conv_w, conv_b, dt_bias, a_log, d_skip_row, norm_w,
                                    l_valid=SSD_CHUNK if l_pad == 0 else l)
    y_ssm = y_ssm[:, :l].reshape(m, d_inner)
    conv_new = conv_new[:, SUBLANES - (CONV_WIDTH - 1):]
    n_ssm_heads = d_inner // SSM_HEAD_DIM
    ssm_new = ssm_new.reshape(b, n_ssm_heads, SSM_HEAD_DIM, SSM_D_STATE)

    h1 = _mix(xf, attn.reshape(m, -1), y_ssm, w_gate, b_gate, w_att, w_ssm, w_out, ln1_g, ln1_b,
              alpha=alpha, tm=tm)
    y = _peer(h1, wq_t, keys1, keys2, u_exp, v_exp, ln2_g, ln2_b, alpha=alpha, tt=tt, i1_per_step=8)
    state = (r3(k).reshape(b, l, ATT_KV_HEADS, ATT_HEAD_DIM), r3(v).reshape(b, l, ATT_KV_HEADS, ATT_HEAD_DIM),
             r3(ki), conv_new, ssm_new)
    return y.reshape(b, l, d), state


def kernel(x_prompt, x_sample, cache_k, cache_v, cache_kidx, state_conv, state_ssm, w_in, b_gate, conv_w, conv_b, dt_bias, a_log, d_skip, ssm_norm_w, w_attn_br, w_ssm_br, w_out, ln1_g, ln1_b, peer_wq, peer_keys1, peer_keys2, peer_u, peer_v, ln2_g, ln2_b):
    depth = w_in.shape[0]
    d_model = x_prompt.shape[2]
    d_inner = ssm_norm_w.shape[1]
    conv_dim = conv_w.shape[2]
    n_ssm_heads = a_log.shape[1]
    alpha = (2.0 * depth) ** 0.25
    pos_p = np.arange(x_prompt.shape[1])
    pos_s = cache_k.shape[2] + np.arange(x_sample.shape[1])

    hp, hs = x_prompt, x_sample
    new_p, new_s = [], []
    for l in range(depth):
        w_a, w_xbc, w_z, w_gate = _split_w_in(w_in[l], d_model, d_inner, conv_dim, n_ssm_heads)
        row = lambda a: a.reshape(1, -1)
        wts = (w_a, w_xbc, w_z, w_gate, row(b_gate[l]),
               jnp.pad(conv_w[l], ((0, SUBLANES - CONV_WIDTH), (0, 0))), row(conv_b[l]),
               _pad_cols(row(dt_bias[l]), LANES), _pad_cols(row(a_log[l]), LANES),
               row(jnp.repeat(d_skip[l], SSM_HEAD_DIM)), row(ssm_norm_w[l]),
               w_attn_br[l].astype(BF16), w_ssm_br[l].astype(BF16), w_out[l].astype(BF16),
               row(ln1_g[l]), row(ln1_b[l]),
               _transpose_cast(peer_wq[l], BF16, 512), peer_keys1[l], peer_keys2[l],
               peer_u[l].astype(BF16), _transpose_cast(peer_v[l], BF16, 512), row(ln2_g[l]), row(ln2_b[l]), alpha)
        hp, sp = _stream(hp, pos_p, None, wts)
        past = (cache_k[l], cache_v[l], cache_kidx[l], state_conv[l], state_ssm[l])
        hs, ss = _stream(hs, pos_s, past, wts)
        new_p.append(sp)
        new_s.append(ss)

    stack = lambda lst, i: jnp.stack([e[i] for e in lst], axis=0)
    return (hp, hs) + tuple(stack(new_p, i) for i in range(5)) + tuple(stack(new_s, i) for i in range(5))
```
